```python
import math
import jax, jax.numpy as jnp
from jax import lax
import numpy as np

D_MODEL = 1024
BATCH = 2
SEQ = 16384
DEPTH = 1
DEC_BATCH = 4
DEC_SEQ = 8192
PAST_LEN = 128

DA_HEADS = 8
DA_HEAD_DIM = 64
DA_V_DIM = 2 * DA_HEAD_DIM
DA_QK_W = DA_HEADS * 2 * DA_HEAD_DIM
DA_V_W = DA_HEADS * DA_V_DIM
MLA_HEADS = 16
MLA_NOPE = 64
MLA_ROPE = 32
MLA_V = 64
MLA_Q_RANK = 384
MLA_KV_RANK = 256
N_EXPERTS = 32
TOP_K = 4
D_EXPERT = 1024
SWIGLU_LIMIT = 7.0
SWIGLU_ALPHA = 1.702
ROPE_THETA = 10000.0
EPS = 1e-6
Q_BLOCK = 128
N_MOD = 6
SPLIT_SIZES = (DA_QK_W, DA_QK_W, DA_V_W, MLA_Q_RANK, MLA_KV_RANK, MLA_ROPE, D_MODEL, D_MODEL)
IN_W = sum(SPLIT_SIZES)

kernel_name = 'hybrid_diffattn_mla_moe_encoder'


def rms_norm(x, w):
    xf = x.astype(jnp.float32)
    y = xf * lax.rsqrt(jnp.mean(xf * xf, axis=-1, keepdims=True) + EPS)
    return (y * w.astype(jnp.float32)).astype(x.dtype)


def rope_tables(seq, dim, dtype):
    inv = 1.0 / (ROPE_THETA ** (jnp.arange(0, dim, 2, dtype=jnp.float32) / dim))
    ang = jnp.arange(seq, dtype=jnp.float32)[:, None] * inv[None, :]
    ang = jnp.concatenate([ang, ang], axis=-1)
    return jnp.cos(ang).astype(dtype), jnp.sin(ang).astype(dtype)


def apply_rope(x, cos, sin):
    half = x.shape[-1] // 2
    rot = jnp.concatenate([-x[..., half:], x[..., :half]], axis=-1)
    shp = (1, cos.shape[0]) + (1,) * (x.ndim - 3) + (cos.shape[1],)
    return x * cos.reshape(shp) + rot * sin.reshape(shp)


def to_blocks(q):
    b, s = q.shape[:2]
    q = q.reshape((b, s // Q_BLOCK, Q_BLOCK) + q.shape[2:])
    return jnp.moveaxis(q, 1, 0)


def from_blocks(o):
    o = jnp.moveaxis(o, 0, 1)
    return o.reshape((o.shape[0], o.shape[1] * o.shape[2]) + o.shape[3:])


def diff_attention(q, k, v, lam, sub_w, lambda_init):
    scale = DA_HEAD_DIM ** -0.5

    def block(qb):
        s = jnp.einsum('bqhmd,bkhmd->bhmqk', qb, k, preferred_element_type=jnp.float32) * scale
        p = jax.nn.softmax(s, axis=-1)
        a = p[:, :, 0] - lam * p[:, :, 1]
        return jnp.einsum('bhqk,bkhe->bqhe', a.astype(v.dtype), v)

    o = from_blocks(lax.map(block, to_blocks(q)))
    o = rms_norm(o, sub_w) * (1.0 - lambda_init)
    return o.reshape(o.shape[0], o.shape[1], DA_V_W)


def mla_attention(c_q, c_kv, k_rope, q_norm_w, kv_norm_w, w_uq, w_ukv, cos, sin):
    b, s = c_q.shape[:2]
    q = (rms_norm(c_q, q_norm_w) @ w_uq).reshape(b, s, MLA_HEADS, MLA_NOPE + MLA_ROPE)
    q_nope = q[..., :MLA_NOPE]
    q_pe = apply_rope(q[..., MLA_NOPE:], cos, sin)
    kv = (rms_norm(c_kv, kv_norm_w) @ w_ukv).reshape(b, s, MLA_HEADS, MLA_NOPE + MLA_V)
    k_nope = kv[..., :MLA_NOPE]
    v = kv[..., MLA_NOPE:]
    k_pe = apply_rope(k_rope, cos, sin)
    scale = (MLA_NOPE + MLA_ROPE) ** -0.5

    def block(qs):
        qn, qp = qs
        sc = (jnp.einsum('bqhd,bkhd->bhqk', qn, k_nope, preferred_element_type=jnp.float32)
              + jnp.einsum('bqhr,bkr->bhqk', qp, k_pe, preferred_element_type=jnp.float32)) * scale
        p = jax.nn.softmax(sc, axis=-1)
        return jnp.einsum('bhqk,bkhe->bqhe', p.astype(v.dtype), v)

    o = from_blocks(lax.map(block, (to_blocks(q_nope), to_blocks(q_pe))))
    return o.reshape(b, s, MLA_HEADS * MLA_V)


def token_mixers(h, lambda_init, w_in, lq1, lk1, lq2, lk2, da_subln, mla_q_norm, mla_kv_norm,
                 mla_w_uq, mla_w_ukv, w_branch_a, w_branch_b, w_out):
    b, s, _ = h.shape
    z = h @ w_in
    offsets = np.cumsum(SPLIT_SIZES)[:-1].tolist()
    q_da, k_da, v_da, c_q, c_kv, k_rope, gate_a, gate_b = jnp.split(z, offsets, axis=-1)
    cos_a, sin_a = rope_tables(s, DA_HEAD_DIM, h.dtype)
    q_da = apply_rope(q_da.reshape(b, s, DA_HEADS, 2, DA_HEAD_DIM), cos_a, sin_a)
    k_da = apply_rope(k_da.reshape(b, s, DA_HEADS, 2, DA_HEAD_DIM), cos_a, sin_a)
    v_da = v_da.reshape(b, s, DA_HEADS, DA_V_DIM)
    lam = (jnp.exp(jnp.sum(lq1.astype(jnp.float32) * lk1.astype(jnp.float32)))
           - jnp.exp(jnp.sum(lq2.astype(jnp.float32) * lk2.astype(jnp.float32))) + lambda_init)
    o_a = diff_attention(q_da, k_da, v_da, lam, da_subln, lambda_init)
    cos_b, sin_b = rope_tables(s, MLA_ROPE, h.dtype)
    o_b = mla_attention(c_q, c_kv, k_rope, mla_q_norm, mla_kv_norm, mla_w_uq, mla_w_ukv, cos_b, sin_b)
    merged = jax.nn.sigmoid(gate_a) * (o_a @ w_branch_a) + jax.nn.sigmoid(gate_b) * (o_b @ w_branch_b)
    return merged @ w_out


def moe_ffn(h, w_router, b_router, w_gate_up, b_gate_up, w_down, b_down):
    b, s, d = h.shape
    t = h.reshape(b * s, d)
    logits = (t @ w_router + b_router).astype(jnp.float32)
    top_v, top_i = lax.top_k(logits, TOP_K)
    top_w = jax.nn.softmax(top_v, axis=-1)
    combine = jnp.sum(jax.nn.one_hot(top_i, N_EXPERTS, dtype=jnp.float32) * top_w[..., None], axis=1)
    acc = jnp.zeros_like(t)
    for e in range(N_EXPERTS):
        gu = t @ w_gate_up[e] + b_gate_up[e]
        glu = jnp.minimum(gu[:, :D_EXPERT], SWIGLU_LIMIT)
        lin = jnp.clip(gu[:, D_EXPERT:], -SWIGLU_LIMIT, SWIGLU_LIMIT)
        y = (glu * jax.nn.sigmoid(SWIGLU_ALPHA * glu) * (lin + 1.0)) @ w_down[e] + b_down[e]
        acc = acc + combine[:, e:e + 1].astype(y.dtype) * y
    return acc.reshape(b, s, d)


def encoder_layer(x, c, lambda_init, w_ada, b_ada, n_pre_mix, n_post_mix, n_pre_ffn, n_post_ffn,
                  w_in, lq1, lk1, lq2, lk2, da_subln, mla_q_norm, mla_kv_norm, mla_w_uq, mla_w_ukv,
                  w_branch_a, w_branch_b, w_out, w_router, b_router, w_gate_up, b_gate_up, w_down, b_down):
    mod = jax.nn.silu(c) @ w_ada + b_ada
    sh1, sc1, g1, sh2, sc2, g2 = [m[:, None, :] for m in jnp.split(mod, N_MOD, axis=-1)]
    h = rms_norm(x, n_pre_mix) * (1.0 + sc1) + sh1
    mix = token_mixers(h, lambda_init, w_in, lq1, lk1, lq2, lk2, da_subln, mla_q_norm, mla_kv_norm,
                       mla_w_uq, mla_w_ukv, w_branch_a, w_branch_b, w_out)
    x = x + g1 * rms_norm(mix, n_post_mix)
    h = rms_norm(x, n_pre_ffn) * (1.0 + sc2) + sh2
    ff = moe_ffn(h, w_router, b_router, w_gate_up, b_gate_up, w_down, b_down)
    return x + g2 * rms_norm(ff, n_post_ffn)


def setup_inputs(seed: int = 0) -> dict:
    key = jax.random.key(seed)
    ks = jax.random.split(key, 32)
    f32 = jnp.float32

    def nrm(k, shape, fan_in):
        return jax.random.normal(k, shape, f32) * (fan_in ** -0.5)

    def gain(k, shape):
        return 1.0 + 0.05 * jax.random.normal(k, shape, f32)

    L = DEPTH
    return {
        'x_prompt': jax.random.normal(ks[0], (BATCH, SEQ, D_MODEL), f32),
        'x_sample': jax.random.normal(ks[1], (DEC_BATCH, DEC_SEQ, D_MODEL), f32),
        'c_prompt': jax.random.normal(ks[2], (BATCH, D_MODEL), f32),
        'c_sample': jax.random.normal(ks[3], (DEC_BATCH, D_MODEL), f32),
        'w_ada': nrm(ks[4], (L, D_MODEL, N_MOD * D_MODEL), D_MODEL) * 0.5,
        'b_ada': 0.02 * jax.random.normal(ks[5], (L, N_MOD * D_MODEL), f32),
        'norm_pre_mix': gain(ks[6], (L, D_MODEL)),
        'norm_post_mix': gain(ks[7], (L, D_MODEL)),
        'norm_pre_ffn': gain(ks[8], (L, D_MODEL)),
        'norm_post_ffn': gain(ks[9], (L, D_MODEL)),
        'w_in': nrm(ks[10], (L, D_MODEL, IN_W), D_MODEL),
        'da_lambda_q1': 0.1 * jax.random.normal(ks[11], (L, DA_HEAD_DIM), f32),
        'da_lambda_k1': 0.1 * jax.random.normal(ks[12], (L, DA_HEAD_DIM), f32),
        'da_lambda_q2': 0.1 * jax.random.normal(ks[13], (L, DA_HEAD_DIM), f32),
        'da_lambda_k2': 0.1 * jax.random.normal(ks[14], (L, DA_HEAD_DIM), f32),
        'da_subln': gain(ks[15], (L, DA_V_DIM)),
        'mla_q_norm': gain(ks[16], (L, MLA_Q_RANK)),
        'mla_kv_norm': gain(ks[17], (L, MLA_KV_RANK)),
        'mla_w_uq': nrm(ks[18], (L, MLA_Q_RANK, MLA_HEADS * (MLA_NOPE + MLA_ROPE)), MLA_Q_RANK),
        'mla_w_ukv': nrm(ks[19], (L, MLA_KV_RANK, MLA_HEADS * (MLA_NOPE + MLA_V)), MLA_KV_RANK),
        'w_branch_a': nrm(ks[20], (L, DA_V_W, D_MODEL), DA_V_W),
        'w_branch_b': nrm(ks[21], (L, MLA_HEADS * MLA_V, D_MODEL), MLA_HEADS * MLA_V),
        'w_out': nrm(ks[22], (L, D_MODEL, D_MODEL), D_MODEL),
        'w_router': nrm(ks[23], (L, D_MODEL, N_EXPERTS), D_MODEL),
        'b_router': 0.01 * jax.random.normal(ks[24], (L, N_EXPERTS), f32),
        'w_gate_up': nrm(ks[25], (L, N_EXPERTS, D_MODEL, 2 * D_EXPERT), D_MODEL),
        'b_gate_up': 0.01 * jax.random.normal(ks[26], (L, N_EXPERTS, 2 * D_EXPERT), f32),
        'w_down': nrm(ks[27], (L, N_EXPERTS, D_EXPERT, D_MODEL), D_EXPERT),
        'b_down': 0.01 * jax.random.normal(ks[28], (L, N_EXPERTS, D_MODEL), f32),
    }


def reference(x_prompt, x_sample, c_prompt, c_sample, w_ada, b_ada, norm_pre_mix, norm_post_mix,
              norm_pre_ffn, norm_post_ffn, w_in, da_lambda_q1, da_lambda_k1, da_lambda_q2, da_lambda_k2,
              da_subln, mla_q_norm, mla_kv_norm, mla_w_uq, mla_w_ukv, w_branch_a, w_branch_b, w_out,
              w_router, b_router, w_gate_up, b_gate_up, w_down, b_down):
    xp = x_prompt
    xs = x_sample
    for l in range(DEPTH):
        lambda_init = 0.8 - 0.6 * math.exp(-0.3 * l)
        lp = (w_ada[l], b_ada[l], norm_pre_mix[l], norm_post_mix[l], norm_pre_ffn[l], norm_post_ffn[l],
              w_in[l], da_lambda_q1[l], da_lambda_k1[l], da_lambda_q2[l], da_lambda_k2[l], da_subln[l],
              mla_q_norm[l], mla_kv_norm[l], mla_w_uq[l], mla_w_ukv[l], w_branch_a[l], w_branch_b[l],
              w_out[l], w_router[l], b_router[l], w_gate_up[l], b_gate_up[l], w_down[l], b_down[l])
        xp = encoder_layer(xp, c_prompt, lambda_init, *lp)
        xs = encoder_layer(xs, c_sample, lambda_init, *lp)
    return (xp, xs)
```

```python
import functools
import math

import jax
import jax.numpy as jnp
from jax import lax
from jax.experimental import pallas as pl
from jax.experimental.pallas import tpu as pltpu

D_MODEL = 1024
DA_HEADS = 8
DA_HEAD_DIM = 64
DA_V_DIM = 128
MLA_HEADS = 16
MLA_NOPE = 64
MLA_ROPE = 32
MLA_V = 64
MLA_Q_RANK = 384
MLA_KV_RANK = 256
N_EXPERTS = 32
TOP_K = 4
D_EXPERT = 1024
SWIGLU_LIMIT = 7.0
SWIGLU_ALPHA = 1.702
ROPE_THETA = 10000.0
EPS = 1e-6
N_MOD = 6

LANES = 128
BF16_SUBLANES = 16
VMEM_LIMIT = 56 * 1024 * 1024
LOG2E = 1.4426950408889634

_QKV0, _GATE0, _CQ0, _CKV0, _KR0, _WP = 0, 3072, 5120, 5504, 5760, 5888

_F32 = jnp.float32
_BF16 = jnp.bfloat16
_NT = (((1,), (1,)), ((), ()))


def _dot(a, b):
    return jnp.dot(a, b, preferred_element_type=_F32)


def _dot_nt(a, b):
    return lax.dot_general(a, b, _NT, preferred_element_type=_F32)


def _rms(x, w):
    return x * lax.rsqrt(jnp.mean(x * x, axis=-1, keepdims=True) + EPS) * w


def _const_spec(shape):
    nd = len(shape)
    return pl.BlockSpec(shape, lambda *_: (0,) * nd, pipeline_mode=pl.Buffered(1))


def _params(sem):
    return pltpu.CompilerParams(dimension_semantics=sem, vmem_limit_bytes=VMEM_LIMIT)


def _ada_kernel(c_ref, w_ref, b_ref, lq1_ref, lk1_ref, lq2_ref, lk2_ref, mod_ref, lam_ref, *, lambda_init):
    c = c_ref[...]
    s = c * jax.nn.sigmoid(c)
    mod_ref[...] = jnp.dot(s, w_ref[...], preferred_element_type=_F32,
                           precision=lax.Precision.HIGHEST) + b_ref[...]
    d1 = jnp.sum(lq1_ref[...] * lk1_ref[...], axis=-1, keepdims=True)
    d2 = jnp.sum(lq2_ref[...] * lk2_ref[...], axis=-1, keepdims=True)
    lam = jnp.exp(d1) - jnp.exp(d2) + lambda_init
    lam_ref[...] = jnp.broadcast_to(lam, lam_ref.shape)


def _ada(c_all, w_ada, b_ada, lq1, lk1, lq2, lk2, lambda_init):
    rows = c_all.shape[0]
    ncol = w_ada.shape[1]
    tn = D_MODEL
    vec = lambda r: r.reshape(1, -1)
    small = pl.BlockSpec((1, DA_HEAD_DIM), lambda j: (0, 0))
    return pl.pallas_call(
        functools.partial(_ada_kernel, lambda_init=lambda_init),
        grid=(ncol // tn,),
        in_specs=[pl.BlockSpec((rows, D_MODEL), lambda j: (0, 0)),
                  pl.BlockSpec((D_MODEL, tn), lambda j: (0, j)),
                  pl.BlockSpec((1, tn), lambda j: (0, j)),
                  small, small, small, small],
        out_specs=[pl.BlockSpec((rows, tn), lambda j: (0, j)),
                   pl.BlockSpec((8, LANES), lambda j: (0, 0))],
        out_shape=[jax.ShapeDtypeStruct((rows, ncol), _F32),
                   jax.ShapeDtypeStruct((8, LANES), _F32)],
        compiler_params=_params(("arbitrary",)),
        name="ada",
    )(c_all, w_ada, vec(b_ada), vec(lq1), vec(lk1), vec(lq2), vec(lk2))


def _rope128(x, cos, sin_signed, first_half, shift_lo, shift_hi):
    r = jnp.where(first_half, pltpu.roll(x, shift_lo, 1), pltpu.roll(x, shift_hi, 1))
    return x * cos + r * sin_signed


def _inproj_kernel(x_ref, mod_ref, n1_ref, wp_ref, qn_ref, kvn_ref, wuq_ref, wuk_ref, wuv_ref,
                   cosa_ref, sina_ref, cosm_ref, sinm_ref,
                   qda_ref, kda_ref, vtda_ref, qm_ref, km_ref, vtm_ref, ga_ref, gb_ref,
                   *, da_scale, mla_scale):
    x = x_ref[0]
    mod = mod_ref[0]
    sh1 = mod[:, 0:D_MODEL]
    sc1 = mod[:, D_MODEL:2 * D_MODEL]
    h = _rms(x, n1_ref[...]) * (1.0 + sc1) + sh1
    hb = h.astype(_BF16)

    lane = lax.broadcasted_iota(jnp.int32, (x.shape[0], LANES), 1)
    da_first = (lane % DA_HEAD_DIM) < (DA_HEAD_DIM // 2)
    pe_mid = MLA_NOPE + MLA_ROPE // 2
    mla_first = lane < pe_mid
    cosa, sina = cosa_ref[...], sina_ref[...]
    cosm, sinm = cosm_ref[...], sinm_ref[...]
    half = DA_HEAD_DIM // 2
    mhalf = MLA_ROPE // 2

    for pair in range(DA_HEADS // 2):
        c0 = 2 * pair * LANES
        zq = _dot(hb, wp_ref[:, _QKV0 + c0:_QKV0 + c0 + 2 * LANES])
        zk = _dot(hb, wp_ref[:, _QKV0 + D_MODEL + c0:_QKV0 + D_MODEL + c0 + 2 * LANES])
        zv = _dot(hb, wp_ref[:, _QKV0 + 2 * D_MODEL + c0:_QKV0 + 2 * D_MODEL + c0 + 2 * LANES])
        for t in range(2):
            hd = 2 * pair + t
            sl = slice(t * LANES, (t + 1) * LANES)
            q = _rope128(zq[:, sl], cosa, sina, da_first, LANES - half, half)
            qda_ref[0, hd] = (q * da_scale).astype(_BF16)
            k = _rope128(zk[:, sl], cosa, sina, da_first, LANES - half, half)
            kda_ref[0, hd] = k.astype(_BF16)
            vtda_ref[0, hd] = zv[:, sl].T.astype(_BF16)

    for blk in range(2 * D_MODEL // 512):
        z = _dot(hb, wp_ref[:, _GATE0 + blk * 512:_GATE0 + (blk + 1) * 512])
        g = jax.nn.sigmoid(z).astype(_BF16)
        if blk < D_MODEL // 512:
            ga_ref[0, :, blk * 512:(blk + 1) * 512] = g
        else:
            o = blk * 512 - D_MODEL
            gb_ref[0, :, o:o + 512] = g

    cq = _dot(hb, wp_ref[:, _CQ0:_CQ0 + MLA_Q_RANK])
    cqn = _rms(cq, qn_ref[...]).astype(_BF16)
    for pair in range(MLA_HEADS // 2):
        z = _dot(cqn, wuq_ref[:, 2 * pair * LANES:(2 * pair + 2) * LANES])
        for t in range(2):
            q = _rope128(z[:, t * LANES:(t + 1) * LANES], cosm, sinm, mla_first, LANES - mhalf, mhalf)
            qm_ref[0, 2 * pair + t] = (q * mla_scale).astype(_BF16)

    ckv = _dot(hb, wp_ref[:, _CKV0:_CKV0 + MLA_KV_RANK])
    ckvn = _rms(ckv, kvn_ref[...]).astype(_BF16)
    kr = _dot(hb, wp_ref[:, _KR0:_KR0 + LANES])
    kr = _rope128(kr, cosm, sinm, mla_first, LANES - mhalf, mhalf)
    for pair in range(MLA_HEADS // 2):
        zk = _dot(ckvn, wuk_ref[:, 2 * pair * LANES:(2 * pair + 2) * LANES])
        for t in range(2):
            km_ref[0, 2 * pair + t] = (zk[:, t * LANES:(t + 1) * LANES] + kr).astype(_BF16)
        vt = _dot(ckvn, wuv_ref[:, pair * LANES:(pair + 1) * LANES]).T.astype(_BF16)
        vtm_ref[0, 2 * pair] = vt[0:MLA_V]
        vtm_ref[0, 2 * pair + 1] = vt[MLA_V:2 * MLA_V]


def _inproj(x, mod3, n1, wp, qn, kvn, wuq, wuk, wuv, cosa, sina, cosm, sinm):
    b, s, _ = x.shape
    tm = min(256, s)
    bf = lambda *shape: jax.ShapeDtypeStruct(shape, _BF16)
    tok = pl.BlockSpec((1, tm, D_MODEL), lambda bi, i: (bi, i, 0))
    tab = pl.BlockSpec((tm, LANES), lambda bi, i: (i, 0))
    hrow = lambda nh: pl.BlockSpec((1, nh, tm, LANES), lambda bi, i: (bi, 0, i, 0))
    hcol = lambda nh, r: pl.BlockSpec((1, nh, r, tm), lambda bi, i: (bi, 0, 0, i))
    kern = functools.partial(_inproj_kernel,
                             da_scale=DA_HEAD_DIM ** -0.5 * LOG2E,
                             mla_scale=(MLA_NOPE + MLA_ROPE) ** -0.5 * LOG2E)
    return pl.pallas_call(
        kern,
        grid=(b, s // tm),
        in_specs=[tok,
                  pl.BlockSpec((1, 1, N_MOD * D_MODEL), lambda bi, i: (bi, 0, 0)),
                  _const_spec((1, D_MODEL)),
                  _const_spec((D_MODEL, _WP)),
                  _const_spec((1, MLA_Q_RANK)),
                  _const_spec((1, MLA_KV_RANK)),
                  _const_spec((MLA_Q_RANK, MLA_HEADS * LANES)),
                  _const_spec((MLA_KV_RANK, MLA_HEADS * LANES)),
                  _const_spec((MLA_KV_RANK, MLA_HEADS * MLA_V)),
                  tab, tab, tab, tab],
        out_specs=[hrow(DA_HEADS), hrow(DA_HEADS), hcol(DA_HEADS, DA_V_DIM),
                   hrow(MLA_HEADS), hrow(MLA_HEADS), hcol(MLA_HEADS, MLA_V), tok, tok],
        out_shape=[bf(b, DA_HEADS, s, LANES), bf(b, DA_HEADS, s, LANES), bf(b, DA_HEADS, DA_V_DIM, s),
                   bf(b, MLA_HEADS, s, LANES), bf(b, MLA_HEADS, s, LANES), bf(b, MLA_HEADS, MLA_V, s),
                   bf(b, s, D_MODEL), bf(b, s, D_MODEL)],
        compiler_params=_params(("parallel", "parallel")),
        name="inproj",
    )(x, mod3, n1, wp, qn, kvn, wuq, wuk, wuv, cosa, sina, cosm, sinm)


def _flash_step(k, w, vt, m_ref, acc_ref, hd):
    st = _dot_nt(k, w)
    m_old = m_ref[hd]
    m_new = jnp.maximum(m_old, jnp.max(st, axis=0, keepdims=True))
    alpha = jnp.exp2(m_old - m_new)
    p = jnp.exp2(st - m_new).astype(_BF16)
    ones = jnp.ones((BF16_SUBLANES, vt.shape[1]), _BF16)
    pv = _dot(jnp.concatenate([vt, ones], axis=0), p)
    acc_ref[hd] = alpha * acc_ref[hd] + pv
    m_ref[hd] = m_new


def _da_kernel(lam_ref, q_ref, k_ref, vt_ref, sub_ref, o_ref, w_ref, m_ref, acc_ref, *, out_scale):
    j = pl.program_id(2)
    tq = q_ref.shape[2]

    @pl.when(j == 0)
    def _():
        lane = lax.broadcasted_iota(jnp.int32, (tq, LANES), 1)
        first = lane < DA_HEAD_DIM
        for hd in range(DA_HEADS):
            q = q_ref[0, hd]
            w_ref[hd, 0:tq] = jnp.where(first, q, jnp.zeros_like(q))
            w_ref[hd, tq:2 * tq] = jnp.where(first, jnp.zeros_like(q), q)
        m_ref[...] = jnp.full(m_ref.shape, -jnp.inf, _F32)
        acc_ref[...] = jnp.zeros(acc_ref.shape, _F32)

    for hd in range(DA_HEADS):
        _flash_step(k_ref[0, hd], w_ref[hd], vt_ref[0, hd], m_ref, acc_ref, hd)

    @pl.when(j == pl.num_programs(2) - 1)
    def _():
        lam = lam_ref[0]
        for hd in range(DA_HEADS):
            a = acc_ref[hd]
            o = a[0:DA_V_DIM] / a[DA_V_DIM:DA_V_DIM + 1]
            od = (o[:, 0:tq] - lam * o[:, tq:2 * tq]).T
            o_ref[0, :, hd * LANES:(hd + 1) * LANES] = (_rms(od, sub_ref[...]) * out_scale).astype(o_ref.dtype)


def _da_attention(lam1, q, k, vt, subln, lambda_init):
    b, nh, s, _ = q.shape
    tq, tk = min(256, s), min(512, s)
    return pl.pallas_call(
        functools.partial(_da_kernel, out_scale=1.0 - lambda_init),
        grid=(b, s // tq, s // tk),
        in_specs=[pl.BlockSpec(memory_space=pltpu.SMEM),
                  pl.BlockSpec((1, nh, tq, LANES), lambda bi, i, j: (bi, 0, i, 0)),
                  pl.BlockSpec((1, nh, tk, LANES), lambda bi, i, j: (bi, 0, j, 0)),
                  pl.BlockSpec((1, nh, DA_V_DIM, tk), lambda bi, i, j: (bi, 0, 0, j)),
                  pl.BlockSpec((1, DA_V_DIM), lambda bi, i, j: (0, 0))],
        out_specs=pl.BlockSpec((1, tq, D_MODEL), lambda bi, i, j: (bi, i, 0)),
        out_shape=jax.ShapeDtypeStruct((b, s, D_MODEL), _BF16),
        scratch_shapes=[pltpu.VMEM((nh, 2 * tq, LANES), _BF16),
                        pltpu.VMEM((nh, 1, 2 * tq), _F32),
                        pltpu.VMEM((nh, DA_V_DIM + BF16_SUBLANES, 2 * tq), _F32)],
        compiler_params=_params(("parallel", "parallel", "arbitrary")),
        name="da_attention",
    )(lam1, q, k, vt, subln)


def _mla_kernel(q_ref, k_ref, vt_ref, o_ref, m_ref, acc_ref):
    j = pl.program_id(2)

    @pl.when(j == 0)
    def _():
        m_ref[...] = jnp.full(m_ref.shape, -jnp.inf, _F32)
        acc_ref[...] = jnp.zeros(acc_ref.shape, _F32)

    for hd in range(MLA_HEADS):
        _flash_step(k_ref[0, hd], q_ref[0, hd], vt_ref[0, hd], m_ref, acc_ref, hd)

    @pl.when(j == pl.num_programs(2) - 1)
    def _():
        for pair in range(MLA_HEADS // 2):
            outs = []
            for hd in (2 * pair, 2 * pair + 1):
                a = acc_ref[hd]
                outs.append(a[0:MLA_V] / a[MLA_V:MLA_V + 1])
            o_ref[0, :, pair * LANES:(pair + 1) * LANES] = jnp.concatenate(outs, axis=0).T.astype(o_ref.dtype)


def _mla_attention(q, k, vt):
    b, nh, s, _ = q.shape
    tq, tk = min(256, s), min(512, s)
    return pl.pallas_call(
        _mla_kernel,
        grid=(b, s // tq, s // tk),
        in_specs=[pl.BlockSpec((1, nh, tq, LANES), lambda bi, i, j: (bi, 0, i, 0)),
                  pl.BlockSpec((1, nh, tk, LANES), lambda bi, i, j: (bi, 0, j, 0)),
                  pl.BlockSpec((1, nh, MLA_V, tk), lambda bi, i, j: (bi, 0, 0, j))],
        out_specs=pl.BlockSpec((1, tq, D_MODEL), lambda bi, i, j: (bi, i, 0)),
        out_shape=jax.ShapeDtypeStruct((b, s, D_MODEL), _BF16),
        scratch_shapes=[pltpu.VMEM((nh, 1, tq), _F32),
                        pltpu.VMEM((nh, MLA_V + BF16_SUBLANES, tq), _F32)],
        compiler_params=_params(("parallel", "parallel", "arbitrary")),
        name="mla_attention",
    )(q, k, vt)


def _outp_kernel(x_ref, mod_ref, oa_ref, ob_ref, ga_ref, gb_ref, wba_ref, wbb_ref, wo_ref,
                 npm_ref, npf_ref, wr_ref, br_ref, x1_ref, h2_ref, comb_ref):
    mod = mod_ref[0]
    g1 = mod[:, 2 * D_MODEL:3 * D_MODEL]
    sh2 = mod[:, 3 * D_MODEL:4 * D_MODEL]
    sc2 = mod[:, 4 * D_MODEL:5 * D_MODEL]
    merged = (ga_ref[0].astype(_F32) * _dot(oa_ref[0], wba_ref[...])
              + gb_ref[0].astype(_F32) * _dot(ob_ref[0], wbb_ref[...]))
    mix = _dot(merged.astype(_BF16), wo_ref[...])
    x1 = x_ref[0] + g1 * _rms(mix, npm_ref[...])
    x1_ref[0] = x1
    h2 = _rms(x1, npf_ref[...]) * (1.0 + sc2) + sh2
    h2_ref[0] = h2.astype(_BF16)

    logits = jnp.dot(h2, wr_ref[...], preferred_element_type=_F32,
                     precision=lax.Precision.HIGHEST) + br_ref[...]
    lane = lax.broadcasted_iota(jnp.int32, logits.shape, 1)
    vals = logits
    top = None
    den = jnp.zeros((logits.shape[0], 1), _F32)
    comb = jnp.zeros(logits.shape, _F32)
    for _ in range(TOP_K):
        mx = jnp.max(vals, axis=-1, keepdims=True)
        idx = jnp.min(jnp.where(vals == mx, lane, LANES), axis=-1, keepdims=True)
        sel = lane == idx
        if top is None:
            top = mx
        e = jnp.exp(mx - top)
        den = den + e
        comb = comb + jnp.where(sel, e, 0.0)
        vals = jnp.where(sel, -jnp.inf, vals)
    comb_ref[0] = comb / den


def _outp(x, mod3, oa, ob, ga, gb, wba, wbb, wo, npm, npf, wr, br):
    b, s, _ = x.shape
    tm = min(256, s)
    tok = pl.BlockSpec((1, tm, D_MODEL), lambda bi, i: (bi, i, 0))
    sq = _const_spec((D_MODEL, D_MODEL))
    row = _const_spec((1, D_MODEL))
    return pl.pallas_call(
        _outp_kernel,
        grid=(b, s // tm),
        in_specs=[tok, pl.BlockSpec((1, 1, N_MOD * D_MODEL), lambda bi, i: (bi, 0, 0)),
                  tok, tok, tok, tok, sq, sq, sq, row, row,
                  _const_spec((D_MODEL, LANES)), _const_spec((1, LANES))],
        out_specs=[tok, tok, pl.BlockSpec((1, tm, LANES), lambda bi, i: (bi, i, 0))],
        out_shape=[jax.ShapeDtypeStruct((b, s, D_MODEL), _F32),
                   jax.ShapeDtypeStruct((b, s, D_MODEL), _BF16),
                   jax.ShapeDtypeStruct((b, s, LANES), _F32)],
        compiler_params=_params(("parallel", "parallel")),
        name="outp",
    )(x, mod3, oa, ob, ga, gb, wba, wbb, wo, npm, npf, wr, br)


def _moe_kernel(h2_ref, comb_ref, wgu_ref, bgu_ref, wd_ref, bd_ref, x1_ref, mod_ref, npo_ref, o_ref, acc_ref):
    e = pl.program_id(2)

    @pl.when(e == 0)
    def _():
        acc_ref[...] = jnp.zeros(acc_ref.shape, _F32)

    gu = _dot(h2_ref[0], wgu_ref[0]) + bgu_ref[0]
    glu = jnp.minimum(gu[:, :D_EXPERT], SWIGLU_LIMIT)
    lin = jnp.clip(gu[:, D_EXPERT:], -SWIGLU_LIMIT, SWIGLU_LIMIT)
    act = glu * jax.nn.sigmoid(SWIGLU_ALPHA * glu) * (lin + 1.0)
    y = _dot(act.astype(_BF16), wd_ref[0]) + bd_ref[0]
    comb = comb_ref[0]
    lane = lax.broadcasted_iota(jnp.int32, comb.shape, 1)
    c = jnp.sum(jnp.where(lane == e, comb, 0.0), axis=-1, keepdims=True)
    acc_ref[...] += c * y

    @pl.when(e == pl.num_programs(2) - 1)
    def _():
        g2 = mod_ref[0][:, 5 * D_MODEL:6 * D_MODEL]
        o_ref[0] = x1_ref[0] + g2 * _rms(acc_ref[...], npo_ref[...])


def _moe(h2, comb, wgu, bgu, wd, bd, x1, mod3, npo):
    b, s, _ = h2.shape
    tm = min(512, s)
    tok = pl.BlockSpec((1, tm, D_MODEL), lambda bi, i, e: (bi, i, 0))
    return pl.pallas_call(
        _moe_kernel,
        grid=(b, s // tm, N_EXPERTS),
        in_specs=[tok, pl.BlockSpec((1, tm, LANES), lambda bi, i, e: (bi, i, 0)),
                  pl.BlockSpec((1, D_MODEL, 2 * D_EXPERT), lambda bi, i, e: (e, 0, 0)),
                  pl.BlockSpec((1, 1, 2 * D_EXPERT), lambda bi, i, e: (e, 0, 0)),
                  pl.BlockSpec((1, D_EXPERT, D_MODEL), lambda bi, i, e: (e, 0, 0)),
                  pl.BlockSpec((1, 1, D_MODEL), lambda bi, i, e: (e, 0, 0)),
                  tok, pl.BlockSpec((1, 1, N_MOD * D_MODEL), lambda bi, i, e: (bi, 0, 0)),
                  pl.BlockSpec((1, D_MODEL), lambda bi, i, e: (0, 0))],
        out_specs=tok,
        out_shape=jax.ShapeDtypeStruct((b, s, D_MODEL), _F32),
        scratch_shapes=[pltpu.VMEM((tm, D_MODEL), _F32)],
        compiler_params=_params(("parallel", "parallel", "arbitrary")),
        name="moe",
    )(h2, comb, wgu, bgu, wd, bd, x1, mod3, npo)


def _rope_tables(seq, dim):
    inv = 1.0 / (ROPE_THETA ** (jnp.arange(0, dim, 2, dtype=_F32) / dim))
    ang = jnp.arange(seq, dtype=_F32)[:, None] * inv[None, :]
    ang = jnp.concatenate([ang, ang], axis=-1)
    return jnp.cos(ang), jnp.sin(ang)


def _tables(seq):
    cos_a, sin_a = _rope_tables(seq, DA_HEAD_DIM)
    half = DA_HEAD_DIM // 2
    sin_a = jnp.concatenate([-sin_a[:, :half], sin_a[:, half:]], axis=-1)
    cosa = jnp.tile(cos_a, (1, LANES // DA_HEAD_DIM))
    sina = jnp.tile(sin_a, (1, LANES // DA_HEAD_DIM))
    cos_b, sin_b = _rope_tables(seq, MLA_ROPE)
    mh = MLA_ROPE // 2
    sin_b = jnp.concatenate([-sin_b[:, :mh], sin_b[:, mh:]], axis=-1)
    pad = LANES - MLA_NOPE - MLA_ROPE
    cosm = jnp.concatenate([jnp.ones((seq, MLA_NOPE), _F32), cos_b, jnp.ones((seq, pad), _F32)], axis=-1)
    sinm = jnp.concatenate([jnp.zeros((seq, MLA_NOPE), _F32), sin_b, jnp.zeros((seq, pad), _F32)], axis=-1)
    return cosa, sina, cosm, sinm


def _pack_weights(w_in, mla_w_uq, mla_w_ukv):
    o = [0, 1024, 2048, 3072, 3456, 3712, 3744, 4768, 5792]
    qkv, cq, ckv, kr, gates = w_in[:, o[0]:o[3]], w_in[:, o[3]:o[4]], w_in[:, o[4]:o[5]], w_in[:, o[5]:o[6]], w_in[:, o[6]:o[8]]
    kr128 = jnp.pad(kr, ((0, 0), (MLA_NOPE, LANES - MLA_NOPE - MLA_ROPE)))
    wp = jnp.concatenate([qkv, gates, cq, ckv, kr128], axis=1).astype(_BF16)
    qk_dim = MLA_NOPE + MLA_ROPE
    wuq = mla_w_uq.reshape(MLA_Q_RANK, MLA_HEADS, qk_dim)
    wuq = jnp.pad(wuq, ((0, 0), (0, 0), (0, LANES - qk_dim))).reshape(MLA_Q_RANK, MLA_HEADS * LANES).astype(_BF16)
    wukv = mla_w_ukv.reshape(MLA_KV_RANK, MLA_HEADS, MLA_NOPE + MLA_V)
    wuk = jnp.pad(wukv[:, :, :MLA_NOPE], ((0, 0), (0, 0), (0, LANES - MLA_NOPE)))
    wuk = wuk.reshape(MLA_KV_RANK, MLA_HEADS * LANES).astype(_BF16)
    wuv = wukv[:, :, MLA_NOPE:].reshape(MLA_KV_RANK, MLA_HEADS * MLA_V).astype(_BF16)
    return wp, wuq, wuk, wuv


def _layer(x, mod, lam1, lambda_init, p):
    b, s, _ = x.shape
    mod3 = mod.reshape(b, 1, N_MOD * D_MODEL)
    cosa, sina, cosm, sinm = _tables(s)
    qda, kda, vtda, qm, km, vtm, ga, gb = _inproj(
        x, mod3, p["n_pre_mix"], p["wp"], p["q_norm"], p["kv_norm"], p["wuq"], p["wuk"], p["wuv"],
        cosa, sina, cosm, sinm)
    oa = _da_attention(lam1, qda, kda, vtda, p["da_subln"], lambda_init)
    ob = _mla_attention(qm, km, vtm)
    x1, h2, comb = _outp(x, mod3, oa, ob, ga, gb, p["wba"], p["wbb"], p["wo"],
                         p["n_post_mix"], p["n_pre_ffn"], p["wr"], p["br"])
    return _moe(h2, comb, p["wgu"], p["bgu"], p["wd"], p["bd"], x1, mod3, p["n_post_ffn"])


def kernel(x_prompt, x_sample, c_prompt, c_sample, w_ada, b_ada, norm_pre_mix, norm_post_mix, norm_pre_ffn, norm_post_ffn, w_in, da_lambda_q1, da_lambda_k1, da_lambda_q2, da_lambda_k2, da_subln, mla_q_norm, mla_kv_norm, mla_w_uq, mla_w_ukv, w_branch_a, w_branch_b, w_out, w_router, b_router, w_gate_up, b_gate_up, w_down, b_down):
    depth = w_ada.shape[0]
    xp, xs = x_prompt, x_sample
    bp, bs = c_prompt.shape[0], c_sample.shape[0]
    rows = -(-(bp + bs) // 8) * 8
    c_all = jnp.concatenate([c_prompt, c_sample, jnp.zeros((rows - bp - bs, D_MODEL), _F32)], axis=0)
    row = lambda v: v.reshape(1, -1)
    for l in range(depth):
        lambda_init = 0.8 - 0.6 * math.exp(-0.3 * l)
        mod, lam = _ada(c_all, w_ada[l], b_ada[l], da_lambda_q1[l], da_lambda_k1[l],
                        da_lambda_q2[l], da_lambda_k2[l], lambda_init)
        lam1 = lam[0, 0:1]
        wp, wuq, wuk, wuv = _pack_weights(w_in[l], mla_w_uq[l], mla_w_ukv[l])
        p = dict(
            n_pre_mix=row(norm_pre_mix[l]), n_post_mix=row(norm_post_mix[l]),
            n_pre_ffn=row(norm_pre_ffn[l]), n_post_ffn=row(norm_post_ffn[l]),
            wp=wp, wuq=wuq, wuk=wuk, wuv=wuv,
            q_norm=row(mla_q_norm[l]), kv_norm=row(mla_kv_norm[l]), da_subln=row(da_subln[l]),
            wba=w_branch_a[l].astype(_BF16), wbb=w_branch_b[l].astype(_BF16), wo=w_out[l].astype(_BF16),
            wr=jnp.pad(w_router[l], ((0, 0), (0, LANES - N_EXPERTS))),
            br=jnp.pad(row(b_router[l]), ((0, 0), (0, LANES - N_EXPERTS)), constant_values=-1e30),
            wgu=w_gate_up[l].astype(_BF16), bgu=b_gate_up[l].reshape(N_EXPERTS, 1, 2 * D_EXPERT),
            wd=w_down[l].astype(_BF16), bd=b_down[l].reshape(N_EXPERTS, 1, D_MODEL),
        )
        xp = _layer(xp, mod[0:bp], lam1, lambda_init, p)
        xs = _layer(xs, mod[bp:bp + bs], lam1, lambda_init, p)
    return (xp, xs)
```

```python
import functools
import math

import jax
import jax.numpy as jnp
from jax import lax
from jax.experimental import pallas as pl
from jax.experimental.pallas import tpu as pltpu

D_MODEL = 1024
DA_HEADS = 8
DA_HEAD_DIM = 64
DA_V_DIM = 128
MLA_HEADS = 16
MLA_NOPE = 64
MLA_ROPE = 32
MLA_V = 64
MLA_Q_RANK = 384
MLA_KV_RANK = 256
N_EXPERTS = 32
TOP_K = 4
D_EXPERT = 1024
SWIGLU_LIMIT = 7.0
SWIGLU_ALPHA = 1.702
ROPE_THETA = 10000.0
EPS = 1e-6
N_MOD = 6

LANES = 128
BF16_SUBLANES = 16
VMEM_LIMIT = 56 * 1024 * 1024
LOG2E = 1.4426950408889634

_QKV0, _GATE0, _CQ0, _CKV0, _KR0, _WP = 0, 3072, 5120, 5504, 5760, 5888

_F32 = jnp.float32
_BF16 = jnp.bfloat16
_NT = (((1,), (1,)), ((), ()))


def _dot(a, b):
    return jnp.dot(a, b, preferred_element_type=_F32)


def _dot_nt(a, b):
    return lax.dot_general(a, b, _NT, preferred_element_type=_F32)


def _rms(x, w):
    return x * lax.rsqrt(jnp.mean(x * x, axis=-1, keepdims=True) + EPS) * w


def _const_spec(shape):
    nd = len(shape)
    return pl.BlockSpec(shape, lambda *_: (0,) * nd, pipeline_mode=pl.Buffered(1))


def _params(sem):
    return pltpu.CompilerParams(dimension_semantics=sem, vmem_limit_bytes=VMEM_LIMIT)


def _ada_kernel(c_ref, w_ref, b_ref, lq1_ref, lk1_ref, lq2_ref, lk2_ref, mod_ref, lam_ref, *, lambda_init):
    c = c_ref[...]
    s = c * jax.nn.sigmoid(c)
    mod_ref[...] = jnp.dot(s, w_ref[...], preferred_element_type=_F32,
                           precision=lax.Precision.HIGHEST) + b_ref[...]
    d1 = jnp.sum(lq1_ref[...] * lk1_ref[...], axis=-1, keepdims=True)
    d2 = jnp.sum(lq2_ref[...] * lk2_ref[...], axis=-1, keepdims=True)
    lam = jnp.exp(d1) - jnp.exp(d2) + lambda_init
    lam_ref[...] = jnp.broadcast_to(lam, lam_ref.shape)


def _ada(c_all, w_ada, b_ada, lq1, lk1, lq2, lk2, lambda_init):
    rows = c_all.shape[0]
    ncol = w_ada.shape[1]
    tn = D_MODEL
    vec = lambda r: r.reshape(1, -1)
    small = pl.BlockSpec((1, DA_HEAD_DIM), lambda j: (0, 0))
    return pl.pallas_call(
        functools.partial(_ada_kernel, lambda_init=lambda_init),
        grid=(ncol // tn,),
        in_specs=[pl.BlockSpec((rows, D_MODEL), lambda j: (0, 0)),
                  pl.BlockSpec((D_MODEL, tn), lambda j: (0, j)),
                  pl.BlockSpec((1, tn), lambda j: (0, j)),
                  small, small, small, small],
        out_specs=[pl.BlockSpec((rows, tn), lambda j: (0, j)),
                   pl.BlockSpec((8, LANES), lambda j: (0, 0))],
        out_shape=[jax.ShapeDtypeStruct((rows, ncol), _F32),
                   jax.ShapeDtypeStruct((8, LANES), _F32)],
        compiler_params=_params(("arbitrary",)),
        name="ada",
    )(c_all, w_ada, vec(b_ada), vec(lq1), vec(lk1), vec(lq2), vec(lk2))


def _rope128(x, cos, sin_signed, first_half, shift_lo, shift_hi):
    r = jnp.where(first_half, pltpu.roll(x, shift_lo, 1), pltpu.roll(x, shift_hi, 1))
    return x * cos + r * sin_signed


def _inproj_kernel(x_ref, mod_ref, n1_ref, wp_ref, qn_ref, kvn_ref, wuq_ref, wuk_ref, wuv_ref,
                   cosa_ref, sina_ref, cosm_ref, sinm_ref,
                   qda_ref, kda_ref, vtda_ref, qm_ref, km_ref, vtm_ref, ga_ref, gb_ref,
                   *, da_scale, mla_scale):
    x = x_ref[0]
    mod = mod_ref[0]
    sh1 = mod[:, 0:D_MODEL]
    sc1 = mod[:, D_MODEL:2 * D_MODEL]
    h = _rms(x, n1_ref[...]) * (1.0 + sc1) + sh1
    hb = h.astype(_BF16)

    lane = lax.broadcasted_iota(jnp.int32, (x.shape[0], LANES), 1)
    da_first = (lane % DA_HEAD_DIM) < (DA_HEAD_DIM // 2)
    pe_mid = MLA_NOPE + MLA_ROPE // 2
    mla_first = lane < pe_mid
    cosa, sina = cosa_ref[...], sina_ref[...]
    cosm, sinm = cosm_ref[...], sinm_ref[...]
    half = DA_HEAD_DIM // 2
    mhalf = MLA_ROPE // 2

    for pair in range(DA_HEADS // 2):
        c0 = 2 * pair * LANES
        zq = _dot(hb, wp_ref[:, _QKV0 + c0:_QKV0 + c0 + 2 * LANES])
        zk = _dot(hb, wp_ref[:, _QKV0 + D_MODEL + c0:_QKV0 + D_MODEL + c0 + 2 * LANES])
        zv = _dot(hb, wp_ref[:, _QKV0 + 2 * D_MODEL + c0:_QKV0 + 2 * D_MODEL + c0 + 2 * LANES])
        for t in range(2):
            hd = 2 * pair + t
            sl = slice(t * LANES, (t + 1) * LANES)
            q = _rope128(zq[:, sl], cosa, sina, da_first, LANES - half, half)
            qda_ref[0, hd] = (q * da_scale).astype(_BF16)
            k = _rope128(zk[:, sl], cosa, sina, da_first, LANES - half, half)
            kda_ref[0, hd] = k.astype(_BF16)
            vtda_ref[0, hd] = zv[:, sl].T.astype(_BF16)

    for blk in range(2 * D_MODEL // 512):
        z = _dot(hb, wp_ref[:, _GATE0 + blk * 512:_GATE0 + (blk + 1) * 512])
        g = jax.nn.sigmoid(z).astype(_BF16)
        if blk < D_MODEL // 512:
            ga_ref[0, :, blk * 512:(blk + 1) * 512] = g
        else:
            o = blk * 512 - D_MODEL
            gb_ref[0, :, o:o + 512] = g

    cq = _dot(hb, wp_ref[:, _CQ0:_CQ0 + MLA_Q_RANK])
    cqn = _rms(cq, qn_ref[...]).astype(_BF16)
    for pair in range(MLA_HEADS // 2):
        z = _dot(cqn, wuq_ref[:, 2 * pair * LANES:(2 * pair + 2) * LANES])
        for t in range(2):
            q = _rope128(z[:, t * LANES:(t + 1) * LANES], cosm, sinm, mla_first, LANES - mhalf, mhalf)
            qm_ref[0, 2 * pair + t] = (q * mla_scale).astype(_BF16)

    ckv = _dot(hb, wp_ref[:, _CKV0:_CKV0 + MLA_KV_RANK])
    ckvn = _rms(ckv, kvn_ref[...]).astype(_BF16)
    kr = _dot(hb, wp_ref[:, _KR0:_KR0 + LANES])
    kr = _rope128(kr, cosm, sinm, mla_first, LANES - mhalf, mhalf)
    for pair in range(MLA_HEADS // 2):
        zk = _dot(ckvn, wuk_ref[:, 2 * pair * LANES:(2 * pair + 2) * LANES])
        for t in range(2):
            km_ref[0, 2 * pair + t] = (zk[:, t * LANES:(t + 1) * LANES] + kr).astype(_BF16)
        vt = _dot(ckvn, wuv_ref[:, pair * LANES:(pair + 1) * LANES]).T.astype(_BF16)
        vtm_ref[0, 2 * pair] = vt[0:MLA_V]
        vtm_ref[0, 2 * pair + 1] = vt[MLA_V:2 * MLA_V]


def _inproj(x, mod3, n1, wp, qn, kvn, wuq, wuk, wuv, cosa, sina, cosm, sinm):
    b, s, _ = x.shape
    tm = min(256, s)
    bf = lambda *shape: jax.ShapeDtypeStruct(shape, _BF16)
    tok = pl.BlockSpec((1, tm, D_MODEL), lambda bi, i: (bi, i, 0))
    tab = pl.BlockSpec((tm, LANES), lambda bi, i: (i, 0))
    hrow = lambda nh: pl.BlockSpec((1, nh, tm, LANES), lambda bi, i: (bi, 0, i, 0))
    hcol = lambda nh, r: pl.BlockSpec((1, nh, r, tm), lambda bi, i: (bi, 0, 0, i))
    kern = functools.partial(_inproj_kernel,
                             da_scale=DA_HEAD_DIM ** -0.5 * LOG2E,
                             mla_scale=(MLA_NOPE + MLA_ROPE) ** -0.5 * LOG2E)
    return pl.pallas_call(
        kern,
        grid=(b, s // tm),
        in_specs=[tok,
                  pl.BlockSpec((1, 1, N_MOD * D_MODEL), lambda bi, i: (bi, 0, 0)),
                  _const_spec((1, D_MODEL)),
                  _const_spec((D_MODEL, _WP)),
                  _const_spec((1, MLA_Q_RANK)),
                  _const_spec((1, MLA_KV_RANK)),
                  _const_spec((MLA_Q_RANK, MLA_HEADS * LANES)),
                  _const_spec((MLA_KV_RANK, MLA_HEADS * LANES)),
                  _const_spec((MLA_KV_RANK, MLA_HEADS * MLA_V)),
                  tab, tab, tab, tab],
        out_specs=[hrow(DA_HEADS), hrow(DA_HEADS), hcol(DA_HEADS, DA_V_DIM),
                   hrow(MLA_HEADS), hrow(MLA_HEADS), hcol(MLA_HEADS, MLA_V), tok, tok],
        out_shape=[bf(b, DA_HEADS, s, LANES), bf(b, DA_HEADS, s, LANES), bf(b, DA_HEADS, DA_V_DIM, s),
                   bf(b, MLA_HEADS, s, LANES), bf(b, MLA_HEADS, s, LANES), bf(b, MLA_HEADS, MLA_V, s),
                   bf(b, s, D_MODEL), bf(b, s, D_MODEL)],
        compiler_params=_params(("parallel", "parallel")),
        name="inproj",
    )(x, mod3, n1, wp, qn, kvn, wuq, wuk, wuv, cosa, sina, cosm, sinm)


def _flash_heads(n_heads, scores, vt_of, m_ref, acc_ref):
    def stage_a(hd):
        st = scores(hd)
        m_old = m_ref[hd]
        m_new = jnp.maximum(m_old, jnp.max(st, axis=0, keepdims=True))
        m_ref[hd] = m_new
        return st, m_new, jnp.exp2(m_old - m_new)

    def stage_c(hd, p, alpha):
        vt = vt_of(hd)
        ones = jnp.ones((BF16_SUBLANES, vt.shape[1]), _BF16)
        pv = _dot(jnp.concatenate([vt, ones], axis=0), p)
        acc_ref[hd] = alpha * acc_ref[hd] + pv

    a_out, b_out = {}, {}
    for t in range(n_heads + 2):
        if t < n_heads:
            a_out[t] = stage_a(t)
        if 0 <= t - 1 < n_heads:
            st, m_new, alpha = a_out.pop(t - 1)
            b_out[t - 1] = (jnp.exp2(st - m_new).astype(_BF16), alpha)
        if 0 <= t - 2 < n_heads:
            stage_c(t - 2, *b_out.pop(t - 2))


def _da_kernel(lam_ref, q_ref, k_ref, vt_ref, sub_ref, o_ref, w_ref, m_ref, acc_ref, *, out_scale):
    j = pl.program_id(2)
    tq = q_ref.shape[2]

    @pl.when(j == 0)
    def _():
        lane = lax.broadcasted_iota(jnp.int32, (tq, LANES), 1)
        first = lane < DA_HEAD_DIM
        for hd in range(DA_HEADS):
            q = q_ref[0, hd]
            w_ref[hd, 0:tq] = jnp.where(first, q, jnp.zeros_like(q))
            w_ref[hd, tq:2 * tq] = jnp.where(first, jnp.zeros_like(q), q)
        m_ref[...] = jnp.full(m_ref.shape, -jnp.inf, _F32)
        acc_ref[...] = jnp.zeros(acc_ref.shape, _F32)

    _flash_heads(DA_HEADS, lambda hd: _dot_nt(k_ref[0, hd], w_ref[hd]), lambda hd: vt_ref[0, hd], m_ref, acc_ref)

    @pl.when(j == pl.num_programs(2) - 1)
    def _():
        lam = lam_ref[0]
        for hd in range(DA_HEADS):
            a = acc_ref[hd]
            o = a[0:DA_V_DIM] / a[DA_V_DIM:DA_V_DIM + 1]
            od = (o[:, 0:tq] - lam * o[:, tq:2 * tq]).T
            o_ref[0, :, hd * LANES:(hd + 1) * LANES] = (_rms(od, sub_ref[...]) * out_scale).astype(o_ref.dtype)


def _da_attention(lam1, q, k, vt, subln, lambda_init):
    b, nh, s, _ = q.shape
    tq, tk = min(512, s), min(512, s)
    return pl.pallas_call(
        functools.partial(_da_kernel, out_scale=1.0 - lambda_init),
        grid=(b, s // tq, s // tk),
        in_specs=[pl.BlockSpec(memory_space=pltpu.SMEM),
                  pl.BlockSpec((1, nh, tq, LANES), lambda bi, i, j: (bi, 0, i, 0)),
                  pl.BlockSpec((1, nh, tk, LANES), lambda bi, i, j: (bi, 0, j, 0)),
                  pl.BlockSpec((1, nh, DA_V_DIM, tk), lambda bi, i, j: (bi, 0, 0, j)),
                  pl.BlockSpec((1, DA_V_DIM), lambda bi, i, j: (0, 0))],
        out_specs=pl.BlockSpec((1, tq, D_MODEL), lambda bi, i, j: (bi, i, 0)),
        out_shape=jax.ShapeDtypeStruct((b, s, D_MODEL), _BF16),
        scratch_shapes=[pltpu.VMEM((nh, 2 * tq, LANES), _BF16),
                        pltpu.VMEM((nh, 1, 2 * tq), _F32),
                        pltpu.VMEM((nh, DA_V_DIM + BF16_SUBLANES, 2 * tq), _F32)],
        compiler_params=_params(("parallel", "parallel", "arbitrary")),
        name="da_attention",
    )(lam1, q, k, vt, subln)


def _mla_kernel(q_ref, k_ref, vt_ref, o_ref, m_ref, acc_ref):
    j = pl.program_id(2)

    @pl.when(j == 0)
    def _():
        m_ref[...] = jnp.full(m_ref.shape, -jnp.inf, _F32)
        acc_ref[...] = jnp.zeros(acc_ref.shape, _F32)

    _flash_heads(MLA_HEADS, lambda hd: _dot_nt(k_ref[0, hd], q_ref[0, hd]), lambda hd: vt_ref[0, hd], m_ref, acc_ref)

    @pl.when(j == pl.num_programs(2) - 1)
    def _():
        for pair in range(MLA_HEADS // 2):
            outs = []
            for hd in (2 * pair, 2 * pair + 1):
                a = acc_ref[hd]
                outs.append(a[0:MLA_V] / a[MLA_V:MLA_V + 1])
            o_ref[0, :, pair * LANES:(pair + 1) * LANES] = jnp.concatenate(outs, axis=0).T.astype(o_ref.dtype)


def _mla_attention(q, k, vt):
    b, nh, s, _ = q.shape
    tq, tk = min(512, s), min(512, s)
    return pl.pallas_call(
        _mla_kernel,
        grid=(b, s // tq, s // tk),
        in_specs=[pl.BlockSpec((1, nh, tq, LANES), lambda bi, i, j: (bi, 0, i, 0)),
                  pl.BlockSpec((1, nh, tk, LANES), lambda bi, i, j: (bi, 0, j, 0)),
                  pl.BlockSpec((1, nh, MLA_V, tk), lambda bi, i, j: (bi, 0, 0, j))],
        out_specs=pl.BlockSpec((1, tq, D_MODEL), lambda bi, i, j: (bi, i, 0)),
        out_shape=jax.ShapeDtypeStruct((b, s, D_MODEL), _BF16),
        scratch_shapes=[pltpu.VMEM((nh, 1, tq), _F32),
                        pltpu.VMEM((nh, MLA_V + BF16_SUBLANES, tq), _F32)],
        compiler_params=_params(("parallel", "parallel", "arbitrary")),
        name="mla_attention",
    )(q, k, vt)


def _outp_kernel(x_ref, mod_ref, oa_ref, ob_ref, ga_ref, gb_ref, wba_ref, wbb_ref, wo_ref,
                 npm_ref, npf_ref, wr_ref, br_ref,
                 x1_ref, h2_ref, idx_ref, pos_ref, w_ref, cnt_ref, carry_ref):
    first = jnp.logical_and(pl.program_id(0) == 0, pl.program_id(1) == 0)

    @pl.when(first)
    def _():
        carry_ref[...] = jnp.zeros(carry_ref.shape, _F32)

    mod = mod_ref[0]
    g1 = mod[:, 2 * D_MODEL:3 * D_MODEL]
    sh2 = mod[:, 3 * D_MODEL:4 * D_MODEL]
    sc2 = mod[:, 4 * D_MODEL:5 * D_MODEL]
    merged = (ga_ref[0].astype(_F32) * _dot(oa_ref[0], wba_ref[...])
              + gb_ref[0].astype(_F32) * _dot(ob_ref[0], wbb_ref[...]))
    mix = _dot(merged.astype(_BF16), wo_ref[...])
    x1 = x_ref[0] + g1 * _rms(mix, npm_ref[...])
    x1_ref[0] = x1
    h2 = _rms(x1, npf_ref[...]) * (1.0 + sc2) + sh2
    h2_ref[0] = h2

    logits = jnp.dot(h2, wr_ref[...], preferred_element_type=_F32,
                     precision=lax.Precision.HIGHEST) + br_ref[...]
    tm = logits.shape[0]
    lane = lax.broadcasted_iota(jnp.int32, logits.shape, 1)
    vals = logits
    sels, idxs, exps = [], [], []
    top = None
    for _ in range(TOP_K):
        mx = jnp.max(vals, axis=-1, keepdims=True)
        idx = jnp.min(jnp.where(vals == mx, lane, LANES), axis=-1, keepdims=True)
        sel = lane == idx
        top = mx if top is None else top
        sels.append(sel)
        idxs.append(idx)
        exps.append(jnp.exp(mx - top))
        vals = jnp.where(sel, -jnp.inf, vals)
    den = exps[0] + exps[1] + exps[2] + exps[3]

    onehot = jnp.zeros(logits.shape, _F32)
    for sel in sels:
        onehot = onehot + jnp.where(sel, 1.0, 0.0)
    r_i = lax.broadcasted_iota(jnp.int32, (tm, tm), 0)
    c_i = lax.broadcasted_iota(jnp.int32, (tm, tm), 1)
    earlier = jnp.where(r_i > c_i, 1.0, 0.0).astype(_BF16)
    rank = _dot(earlier, onehot.astype(_BF16)) + carry_ref[0:1, :]
    idx4 = jnp.zeros(logits.shape, jnp.int32)
    pos4 = jnp.zeros(logits.shape, jnp.int32)
    w4 = jnp.zeros(logits.shape, _F32)
    for r in range(TOP_K):
        pos = jnp.sum(jnp.where(sels[r], rank, 0.0), axis=-1, keepdims=True)
        idx4 = jnp.where(lane == r, idxs[r], idx4)
        pos4 = jnp.where(lane == r, pos.astype(jnp.int32), pos4)
        w4 = jnp.where(lane == r, exps[r] / den, w4)
    idx_ref[0] = idx4
    pos_ref[0] = pos4
    w_ref[0] = w4
    carry_ref[...] = carry_ref[...] + jnp.sum(onehot, axis=0, keepdims=True)
    cnt_ref[...] = carry_ref[...]


def _outp(x, mod3, oa, ob, ga, gb, wba, wbb, wo, npm, npf, wr, br):
    b, s, _ = x.shape
    tm = min(256, s)
    tok = pl.BlockSpec((1, tm, D_MODEL), lambda bi, i: (bi, i, 0))
    meta = pl.BlockSpec((1, tm, LANES), lambda bi, i: (bi, i, 0))
    sq = _const_spec((D_MODEL, D_MODEL))
    row = _const_spec((1, D_MODEL))
    return pl.pallas_call(
        _outp_kernel,
        grid=(b, s // tm),
        in_specs=[tok, pl.BlockSpec((1, 1, N_MOD * D_MODEL), lambda bi, i: (bi, 0, 0)),
                  tok, tok, tok, tok, sq, sq, sq, row, row,
                  _const_spec((D_MODEL, LANES)), _const_spec((1, LANES))],
        out_specs=[tok, tok, meta, meta, meta, pl.BlockSpec((8, LANES), lambda bi, i: (0, 0))],
        out_shape=[jax.ShapeDtypeStruct((b, s, D_MODEL), _F32),
                   jax.ShapeDtypeStruct((b, s, D_MODEL), _F32),
                   jax.ShapeDtypeStruct((b, s, LANES), jnp.int32),
                   jax.ShapeDtypeStruct((b, s, LANES), jnp.int32),
                   jax.ShapeDtypeStruct((b, s, LANES), _F32),
                   jax.ShapeDtypeStruct((8, LANES), _F32)],
        scratch_shapes=[pltpu.VMEM((8, LANES), _F32)],
        compiler_params=_params(("arbitrary", "arbitrary")),
        name="outp",
    )(x, mod3, oa, ob, ga, gb, wba, wbb, wo, npm, npf, wr, br)


MOE_ROW_TILE = 512
ROUTE_TILE = 256


def _scatter_kernel(dest_ref, h_ref, init_ref, xs_ref, sem):
    del init_ref
    tm = h_ref.shape[0]

    def issue(t, carry):
        for r in range(TOP_K):
            d = dest_ref[TOP_K * t + r]
            pltpu.make_async_copy(h_ref.at[pl.ds(t, 1)], xs_ref.at[pl.ds(d, 1)], sem).start()
        return carry

    lax.fori_loop(0, tm, issue, 0)
    for _ in range(TOP_K):
        pltpu.make_async_copy(h_ref, xs_ref.at[pl.ds(0, tm)], sem).wait()


def _scatter_rows(dest, h2, n_rows):
    t = h2.shape[0]
    tm = min(ROUTE_TILE, t)
    return pl.pallas_call(
        _scatter_kernel,
        grid=(t // tm,),
        in_specs=[pl.BlockSpec((TOP_K * tm,), lambda i: (i,), memory_space=pltpu.SMEM),
                  pl.BlockSpec((tm, D_MODEL), lambda i: (i, 0)),
                  pl.BlockSpec(memory_space=pl.ANY)],
        out_specs=pl.BlockSpec(memory_space=pl.ANY),
        out_shape=jax.ShapeDtypeStruct((n_rows, D_MODEL), _F32),
        scratch_shapes=[pltpu.SemaphoreType.DMA(())],
        input_output_aliases={2: 0},
        compiler_params=_params(("arbitrary",)),
        name="moe_scatter",
    )(dest, h2, jnp.zeros((n_rows, D_MODEL), _F32))


def _experts_kernel(te_ref, xs_ref, wgu_ref, bgu_ref, wd_ref, bd_ref, ys_ref):
    del te_ref
    gu = _dot(xs_ref[...].astype(_BF16), wgu_ref[0]) + bgu_ref[0]
    glu = jnp.minimum(gu[:, :D_EXPERT], SWIGLU_LIMIT)
    lin = jnp.clip(gu[:, D_EXPERT:], -SWIGLU_LIMIT, SWIGLU_LIMIT)
    act = glu * jax.nn.sigmoid(SWIGLU_ALPHA * glu) * (lin + 1.0)
    ys_ref[...] = _dot(act.astype(_BF16), wd_ref[0]) + bd_ref[0]


def _experts(tile_expert, xs, wgu, bgu, wd, bd):
    n_rows = xs.shape[0]
    tr = MOE_ROW_TILE
    rows = pl.BlockSpec((tr, D_MODEL), lambda g, te: (g, 0))
    return pl.pallas_call(
        _experts_kernel,
        grid_spec=pltpu.PrefetchScalarGridSpec(
            num_scalar_prefetch=1,
            grid=(n_rows // tr,),
            in_specs=[rows,
                      pl.BlockSpec((1, D_MODEL, 2 * D_EXPERT), lambda g, te: (te[g], 0, 0)),
                      pl.BlockSpec((1, 1, 2 * D_EXPERT), lambda g, te: (te[g], 0, 0)),
                      pl.BlockSpec((1, D_EXPERT, D_MODEL), lambda g, te: (te[g], 0, 0)),
                      pl.BlockSpec((1, 1, D_MODEL), lambda g, te: (te[g], 0, 0))],
            out_specs=rows),
        out_shape=jax.ShapeDtypeStruct((n_rows, D_MODEL), _F32),
        compiler_params=_params(("arbitrary",)),
        name="moe_experts",
    )(tile_expert, xs, wgu, bgu, wd, bd)


def _combine_kernel(dest_ref, ys_ref, w_ref, x1_ref, mod_ref, npo_ref, o_ref, buf_ref, sem):
    tm = x1_ref.shape[1]

    def issue(t, carry):
        for r in range(TOP_K):
            d = dest_ref[TOP_K * t + r]
            pltpu.make_async_copy(ys_ref.at[pl.ds(d, 1)], buf_ref.at[r, pl.ds(t, 1)], sem).start()
        return carry

    lax.fori_loop(0, tm, issue, 0)
    for r in range(TOP_K):
        pltpu.make_async_copy(ys_ref.at[pl.ds(0, tm)], buf_ref.at[r], sem).wait()
    w = w_ref[0]
    ff = w[:, 0:1] * buf_ref[0]
    for r in range(1, TOP_K):
        ff = ff + w[:, r:r + 1] * buf_ref[r]
    g2 = mod_ref[0][:, 5 * D_MODEL:6 * D_MODEL]
    o_ref[0] = x1_ref[0] + g2 * _rms(ff, npo_ref[...])


def _combine(dest, ys, w4, x1, mod3, npo):
    b, s, _ = x1.shape
    tm = min(ROUTE_TILE, s)
    nt = s // tm
    tok = pl.BlockSpec((1, tm, D_MODEL), lambda bi, i: (bi, i, 0))
    return pl.pallas_call(
        _combine_kernel,
        grid=(b, nt),
        in_specs=[pl.BlockSpec((TOP_K * tm,), lambda bi, i: (bi * nt + i,), memory_space=pltpu.SMEM),
                  pl.BlockSpec(memory_space=pl.ANY),
                  pl.BlockSpec((1, tm, LANES), lambda bi, i: (bi, i, 0)),
                  tok, pl.BlockSpec((1, 1, N_MOD * D_MODEL), lambda bi, i: (bi, 0, 0)),
                  pl.BlockSpec((1, D_MODEL), lambda bi, i: (0, 0))],
        out_specs=tok,
        out_shape=jax.ShapeDtypeStruct((b, s, D_MODEL), _F32),
        scratch_shapes=[pltpu.VMEM((TOP_K, tm, D_MODEL), _F32), pltpu.SemaphoreType.DMA(())],
        compiler_params=_params(("arbitrary", "arbitrary")),
        name="moe_combine",
    )(dest, ys, w4, x1, mod3, npo)


def _moe(h2, idx4, pos4, w4, counts, wgu, bgu, wd, bd, x1, mod3, npo):
    b, s, _ = h2.shape
    t = b * s
    tr = MOE_ROW_TILE
    n_tiles = (t * TOP_K) // tr + N_EXPERTS
    cnt = counts[0, :N_EXPERTS].astype(jnp.int32)
    padded = ((cnt + tr - 1) // tr) * tr
    ends = jnp.cumsum(padded)
    starts = ends - padded
    dest = (starts[idx4[..., :TOP_K]] + pos4[..., :TOP_K]).reshape(t * TOP_K)
    tile_expert = jnp.searchsorted(ends, jnp.arange(n_tiles, dtype=jnp.int32) * tr, side="right")
    tile_expert = jnp.minimum(tile_expert, N_EXPERTS - 1).astype(jnp.int32)
    xs = _scatter_rows(dest, h2.reshape(t, D_MODEL), n_tiles * tr)
    ys = _experts(tile_expert, xs, wgu, bgu, wd, bd)
    return _combine(dest, ys, w4, x1, mod3, npo)


def _rope_tables(seq, dim):
    inv = 1.0 / (ROPE_THETA ** (jnp.arange(0, dim, 2, dtype=_F32) / dim))
    ang = jnp.arange(seq, dtype=_F32)[:, None] * inv[None, :]
    ang = jnp.concatenate([ang, ang], axis=-1)
    return jnp.cos(ang), jnp.sin(ang)


def _tables(seq):
    cos_a, sin_a = _rope_tables(seq, DA_HEAD_DIM)
    half = DA_HEAD_DIM // 2
    sin_a = jnp.concatenate([-sin_a[:, :half], sin_a[:, half:]], axis=-1)
    cosa = jnp.tile(cos_a, (1, LANES // DA_HEAD_DIM))
    sina = jnp.tile(sin_a, (1, LANES // DA_HEAD_DIM))
    cos_b, sin_b = _rope_tables(seq, MLA_ROPE)
    mh = MLA_ROPE // 2
    sin_b = jnp.concatenate([-sin_b[:, :mh], sin_b[:, mh:]], axis=-1)
    pad = LANES - MLA_NOPE - MLA_ROPE
    cosm = jnp.concatenate([jnp.ones((seq, MLA_NOPE), _F32), cos_b, jnp.ones((seq, pad), _F32)], axis=-1)
    sinm = jnp.concatenate([jnp.zeros((seq, MLA_NOPE), _F32), sin_b, jnp.zeros((seq, pad), _F32)], axis=-1)
    return cosa, sina, cosm, sinm


def _pack_weights(w_in, mla_w_uq, mla_w_ukv):
    o = [0, 1024, 2048, 3072, 3456, 3712, 3744, 4768, 5792]
    qkv, cq, ckv, kr, gates = w_in[:, o[0]:o[3]], w_in[:, o[3]:o[4]], w_in[:, o[4]:o[5]], w_in[:, o[5]:o[6]], w_in[:, o[6]:o[8]]
    kr128 = jnp.pad(kr, ((0, 0), (MLA_NOPE, LANES - MLA_NOPE - MLA_ROPE)))
    wp = jnp.concatenate([qkv, gates, cq, ckv, kr128], axis=1).astype(_BF16)
    qk_dim = MLA_NOPE + MLA_ROPE
    wuq = mla_w_uq.reshape(MLA_Q_RANK, MLA_HEADS, qk_dim)
    wuq = jnp.pad(wuq, ((0, 0), (0, 0), (0, LANES - qk_dim))).reshape(MLA_Q_RANK, MLA_HEADS * LANES).astype(_BF16)
    wukv = mla_w_ukv.reshape(MLA_KV_RANK, MLA_HEADS, MLA_NOPE + MLA_V)
    wuk = jnp.pad(wukv[:, :, :MLA_NOPE], ((0, 0), (0, 0), (0, LANES - MLA_NOPE)))
    wuk = wuk.reshape(MLA_KV_RANK, MLA_HEADS * LANES).astype(_BF16)
    wuv = wukv[:, :, MLA_NOPE:].reshape(MLA_KV_RANK, MLA_HEADS * MLA_V).astype(_BF16)
    return wp, wuq, wuk, wuv


def _layer(x, mod, lam1, lambda_init, p):
    b, s, _ = x.shape
    mod3 = mod.reshape(b, 1, N_MOD * D_MODEL)
    cosa, sina, cosm, sinm = _tables(s)
    qda, kda, vtda, qm, km, vtm, ga, gb = _inproj(
        x, mod3, p["n_pre_mix"], p["wp"], p["q_norm"], p["kv_norm"], p["wuq"], p["wuk"], p["wuv"],
        cosa, sina, cosm, sinm)
    oa = _da_attention(lam1, qda, kda, vtda, p["da_subln"], lambda_init)
    ob = _mla_attention(qm, km, vtm)
    x1, h2, idx4, pos4, w4, counts = _outp(x, mod3, oa, ob, ga, gb, p["wba"], p["wbb"], p["wo"],
                                           p["n_post_mix"], p["n_pre_ffn"], p["wr"], p["br"])
    return _moe(h2, idx4, pos4, w4, counts, p["wgu"], p["bgu"], p["wd"], p["bd"], x1, mod3, p["n_post_ffn"])


def kernel(x_prompt, x_sample, c_prompt, c_sample, w_ada, b_ada, norm_pre_mix, norm_post_mix, norm_pre_ffn, norm_post_ffn, w_in, da_lambda_q1, da_lambda_k1, da_lambda_q2, da_lambda_k2, da_subln, mla_q_norm, mla_kv_norm, mla_w_uq, mla_w_ukv, w_branch_a, w_branch_b, w_out, w_router, b_router, w_gate_up, b_gate_up, w_down, b_down):
    depth = w_ada.shape[0]
    xp, xs = x_prompt, x_sample
    bp, bs = c_prompt.shape[0], c_sample.shape[0]
    rows = -(-(bp + bs) // 8) * 8
    c_all = jnp.concatenate([c_prompt, c_sample, jnp.zeros((rows - bp - bs, D_MODEL), _F32)], axis=0)
    row = lambda v: v.reshape(1, -1)
    for l in range(depth):
        lambda_init = 0.8 - 0.6 * math.exp(-0.3 * l)
        mod, lam = _ada(c_all, w_ada[l], b_ada[l], da_lambda_q1[l], da_lambda_k1[l],
                        da_lambda_q2[l], da_lambda_k2[l], lambda_init)
        lam1 = lam[0, 0:1]
        wp, wuq, wuk, wuv = _pack_weights(w_in[l], mla_w_uq[l], mla_w_ukv[l])
        p = dict(
            n_pre_mix=row(norm_pre_mix[l]), n_post_mix=row(norm_post_mix[l]),
            n_pre_ffn=row(norm_pre_ffn[l]), n_post_ffn=row(norm_post_ffn[l]),
            wp=wp, wuq=wuq, wuk=wuk, wuv=wuv,
            q_norm=row(mla_q_norm[l]), kv_norm=row(mla_kv_norm[l]), da_subln=row(da_subln[l]),
            wba=w_branch_a[l].astype(_BF16), wbb=w_branch_b[l].astype(_BF16), wo=w_out[l].astype(_BF16),
            wr=jnp.pad(w_router[l], ((0, 0), (0, LANES - N_EXPERTS))),
            br=jnp.pad(row(b_router[l]), ((0, 0), (0, LANES - N_EXPERTS)), constant_values=-1e30),
            wgu=w_gate_up[l].astype(_BF16), bgu=b_gate_up[l].reshape(N_EXPERTS, 1, 2 * D_EXPERT),
            wd=w_down[l].astype(_BF16), bd=b_down[l].reshape(N_EXPERTS, 1, D_MODEL),
        )
        xp = _layer(xp, mod[0:bp], lam1, lambda_init, p)
        xs = _layer(xs, mod[bp:bp + bs], lam1, lambda_init, p)
    return (xp, xs)
```

```python
import functools
import math

import jax
import jax.numpy as jnp
from jax import lax
from jax.experimental import pallas as pl
from jax.experimental.pallas import tpu as pltpu

D_MODEL = 1024
DA_HEADS = 8
DA_HEAD_DIM = 64
DA_V_DIM = 128
MLA_HEADS = 16
MLA_NOPE = 64
MLA_ROPE = 32
MLA_V = 64
MLA_Q_RANK = 384
MLA_KV_RANK = 256
N_EXPERTS = 32
TOP_K = 4
D_EXPERT = 1024
SWIGLU_LIMIT = 7.0
SWIGLU_ALPHA = 1.702
ROPE_THETA = 10000.0
EPS = 1e-6
N_MOD = 6

LANES = 128
BF16_SUBLANES = 16
VMEM_LIMIT = 56 * 1024 * 1024
LOG2E = 1.4426950408889634

_QKV0, _GATE0, _CQ0, _CKV0, _KR0, _WP = 0, 3072, 5120, 5504, 5760, 5888

_F32 = jnp.float32
_BF16 = jnp.bfloat16
_NT = (((1,), (1,)), ((), ()))


def _dot(a, b):
    return jnp.dot(a, b, preferred_element_type=_F32)


def _dot_nt(a, b):
    return lax.dot_general(a, b, _NT, preferred_element_type=_F32)


def _rms(x, w):
    return x * lax.rsqrt(jnp.mean(x * x, axis=-1, keepdims=True) + EPS) * w


def _const_spec(shape):
    nd = len(shape)
    return pl.BlockSpec(shape, lambda *_: (0,) * nd, pipeline_mode=pl.Buffered(1))


def _params(sem):
    return pltpu.CompilerParams(dimension_semantics=sem, vmem_limit_bytes=VMEM_LIMIT)


def _ada_kernel(c_ref, w_ref, b_ref, lq1_ref, lk1_ref, lq2_ref, lk2_ref, mod_ref, lam_ref, *, lambda_init):
    c = c_ref[...]
    s = c * jax.nn.sigmoid(c)
    mod_ref[...] = jnp.dot(s, w_ref[...], preferred_element_type=_F32,
                           precision=lax.Precision.HIGHEST) + b_ref[...]
    d1 = jnp.sum(lq1_ref[...] * lk1_ref[...], axis=-1, keepdims=True)
    d2 = jnp.sum(lq2_ref[...] * lk2_ref[...], axis=-1, keepdims=True)
    lam = jnp.exp(d1) - jnp.exp(d2) + lambda_init
    lam_ref[...] = jnp.broadcast_to(lam, lam_ref.shape)


def _ada(c_all, w_ada, b_ada, lq1, lk1, lq2, lk2, lambda_init):
    rows = c_all.shape[0]
    ncol = w_ada.shape[1]
    tn = D_MODEL
    vec = lambda r: r.reshape(1, -1)
    small = pl.BlockSpec((1, DA_HEAD_DIM), lambda j: (0, 0))
    return pl.pallas_call(
        functools.partial(_ada_kernel, lambda_init=lambda_init),
        grid=(ncol // tn,),
        in_specs=[pl.BlockSpec((rows, D_MODEL), lambda j: (0, 0)),
                  pl.BlockSpec((D_MODEL, tn), lambda j: (0, j)),
                  pl.BlockSpec((1, tn), lambda j: (0, j)),
                  small, small, small, small],
        out_specs=[pl.BlockSpec((rows, tn), lambda j: (0, j)),
                   pl.BlockSpec((8, LANES), lambda j: (0, 0))],
        out_shape=[jax.ShapeDtypeStruct((rows, ncol), _F32),
                   jax.ShapeDtypeStruct((8, LANES), _F32)],
        compiler_params=_params(("arbitrary",)),
        name="ada",
    )(c_all, w_ada, vec(b_ada), vec(lq1), vec(lk1), vec(lq2), vec(lk2))


def _rope128(x, cos, sin_signed, first_half, shift_lo, shift_hi):
    r = jnp.where(first_half, pltpu.roll(x, shift_lo, 1), pltpu.roll(x, shift_hi, 1))
    return x * cos + r * sin_signed


def _inproj_kernel(x_ref, mod_ref, n1_ref, wp_ref, qn_ref, kvn_ref, wuq_ref, wuk_ref, wuv_ref,
                   cosa_ref, sina_ref, cosm_ref, sinm_ref,
                   qda_ref, kda_ref, vtda_ref, qm_ref, km_ref, vtm_ref, ga_ref, gb_ref,
                   *, da_scale, mla_scale):
    x = x_ref[0]
    mod = mod_ref[0]
    sh1 = mod[:, 0:D_MODEL]
    sc1 = mod[:, D_MODEL:2 * D_MODEL]
    h = _rms(x, n1_ref[...]) * (1.0 + sc1) + sh1
    hb = h.astype(_BF16)

    lane = lax.broadcasted_iota(jnp.int32, (x.shape[0], LANES), 1)
    da_first = (lane % DA_HEAD_DIM) < (DA_HEAD_DIM // 2)
    pe_mid = MLA_NOPE + MLA_ROPE // 2
    mla_first = lane < pe_mid
    cosa, sina = cosa_ref[...], sina_ref[...]
    cosm, sinm = cosm_ref[...], sinm_ref[...]
    half = DA_HEAD_DIM // 2
    mhalf = MLA_ROPE // 2

    for pair in range(DA_HEADS // 2):
        c0 = 2 * pair * LANES
        zq = _dot(hb, wp_ref[:, _QKV0 + c0:_QKV0 + c0 + 2 * LANES])
        zk = _dot(hb, wp_ref[:, _QKV0 + D_MODEL + c0:_QKV0 + D_MODEL + c0 + 2 * LANES])
        zv = _dot(hb, wp_ref[:, _QKV0 + 2 * D_MODEL + c0:_QKV0 + 2 * D_MODEL + c0 + 2 * LANES])
        for t in range(2):
            hd = 2 * pair + t
            sl = slice(t * LANES, (t + 1) * LANES)
            q = _rope128(zq[:, sl], cosa, sina, da_first, LANES - half, half)
            qda_ref[0, hd] = (q * da_scale).astype(_BF16)
            k = _rope128(zk[:, sl], cosa, sina, da_first, LANES - half, half)
            kda_ref[0, hd] = k.astype(_BF16)
            vtda_ref[0, hd] = zv[:, sl].T.astype(_BF16)

    for blk in range(2 * D_MODEL // 512):
        z = _dot(hb, wp_ref[:, _GATE0 + blk * 512:_GATE0 + (blk + 1) * 512])
        g = jax.nn.sigmoid(z).astype(_BF16)
        if blk < D_MODEL // 512:
            ga_ref[0, :, blk * 512:(blk + 1) * 512] = g
        else:
            o = blk * 512 - D_MODEL
            gb_ref[0, :, o:o + 512] = g

    cq = _dot(hb, wp_ref[:, _CQ0:_CQ0 + MLA_Q_RANK])
    cqn = _rms(cq, qn_ref[...]).astype(_BF16)
    for pair in range(MLA_HEADS // 2):
        z = _dot(cqn, wuq_ref[:, 2 * pair * LANES:(2 * pair + 2) * LANES])
        for t in range(2):
            q = _rope128(z[:, t * LANES:(t + 1) * LANES], cosm, sinm, mla_first, LANES - mhalf, mhalf)
            qm_ref[0, 2 * pair + t] = (q * mla_scale).astype(_BF16)

    ckv = _dot(hb, wp_ref[:, _CKV0:_CKV0 + MLA_KV_RANK])
    ckvn = _rms(ckv, kvn_ref[...]).astype(_BF16)
    kr = _dot(hb, wp_ref[:, _KR0:_KR0 + LANES])
    kr = _rope128(kr, cosm, sinm, mla_first, LANES - mhalf, mhalf)
    for pair in range(MLA_HEADS // 2):
        zk = _dot(ckvn, wuk_ref[:, 2 * pair * LANES:(2 * pair + 2) * LANES])
        for t in range(2):
            km_ref[0, 2 * pair + t] = (zk[:, t * LANES:(t + 1) * LANES] + kr).astype(_BF16)
        vt = _dot(ckvn, wuv_ref[:, pair * LANES:(pair + 1) * LANES]).T.astype(_BF16)
        vtm_ref[0, 2 * pair] = vt[0:MLA_V]
        vtm_ref[0, 2 * pair + 1] = vt[MLA_V:2 * MLA_V]


def _inproj(x, mod3, n1, wp, qn, kvn, wuq, wuk, wuv, cosa, sina, cosm, sinm):
    b, s, _ = x.shape
    tm = min(256, s)
    bf = lambda *shape: jax.ShapeDtypeStruct(shape, _BF16)
    tok = pl.BlockSpec((1, tm, D_MODEL), lambda bi, i: (bi, i, 0))
    tab = pl.BlockSpec((tm, LANES), lambda bi, i: (i, 0))
    hrow = lambda nh: pl.BlockSpec((1, nh, tm, LANES), lambda bi, i: (bi, 0, i, 0))
    hcol = lambda nh, r: pl.BlockSpec((1, nh, r, tm), lambda bi, i: (bi, 0, 0, i))
    kern = functools.partial(_inproj_kernel,
                             da_scale=DA_HEAD_DIM ** -0.5 * LOG2E,
                             mla_scale=(MLA_NOPE + MLA_ROPE) ** -0.5 * LOG2E)
    return pl.pallas_call(
        kern,
        grid=(b, s // tm),
        in_specs=[tok,
                  pl.BlockSpec((1, 1, N_MOD * D_MODEL), lambda bi, i: (bi, 0, 0)),
                  _const_spec((1, D_MODEL)),
                  _const_spec((D_MODEL, _WP)),
                  _const_spec((1, MLA_Q_RANK)),
                  _const_spec((1, MLA_KV_RANK)),
                  _const_spec((MLA_Q_RANK, MLA_HEADS * LANES)),
                  _const_spec((MLA_KV_RANK, MLA_HEADS * LANES)),
                  _const_spec((MLA_KV_RANK, MLA_HEADS * MLA_V)),
                  tab, tab, tab, tab],
        out_specs=[hrow(DA_HEADS), hrow(DA_HEADS), hcol(DA_HEADS, DA_V_DIM),
                   hrow(MLA_HEADS), hrow(MLA_HEADS), hcol(MLA_HEADS, MLA_V), tok, tok],
        out_shape=[bf(b, DA_HEADS, s, LANES), bf(b, DA_HEADS, s, LANES), bf(b, DA_HEADS, DA_V_DIM, s),
                   bf(b, MLA_HEADS, s, LANES), bf(b, MLA_HEADS, s, LANES), bf(b, MLA_HEADS, MLA_V, s),
                   bf(b, s, D_MODEL), bf(b, s, D_MODEL)],
        compiler_params=_params(("parallel", "parallel")),
        name="inproj",
    )(x, mod3, n1, wp, qn, kvn, wuq, wuk, wuv, cosa, sina, cosm, sinm)


DA_KEY_SUB, DA_KEY_BLOCK = 512, 2048
MLA_KEY_SUB, MLA_KEY_BLOCK = 1024, 2048
QUERY_TILE = 512


def _flash_items(n_heads, tk, key_sub, scores, vt_of, m_ref, acc_ref):
    sub = min(key_sub, tk)
    items = [(hd, pl.ds(c * sub, sub)) for c in range(tk // sub) for hd in range(n_heads)]

    def stage_a(hd, ks):
        st = scores(hd, ks)
        m_old = m_ref[hd]
        m_new = jnp.maximum(m_old, jnp.max(st, axis=0, keepdims=True))
        m_ref[hd] = m_new
        return st, m_new, jnp.exp2(m_old - m_new)

    def stage_c(hd, ks, p, alpha):
        vt = vt_of(hd, ks)
        ones = jnp.ones((BF16_SUBLANES, vt.shape[1]), _BF16)
        pv = _dot(jnp.concatenate([vt, ones], axis=0), p)
        acc_ref[hd] = alpha * acc_ref[hd] + pv

    a_out, b_out = {}, {}
    for t in range(len(items) + 2):
        if t < len(items):
            a_out[t] = stage_a(*items[t])
        if 0 <= t - 1 < len(items):
            st, m_new, alpha = a_out.pop(t - 1)
            b_out[t - 1] = (jnp.exp2(st - m_new).astype(_BF16), alpha)
        if 0 <= t - 2 < len(items):
            stage_c(*items[t - 2], *b_out.pop(t - 2))


def _da_kernel(lam_ref, q_ref, k_ref, vt_ref, sub_ref, o_ref, w_ref, m_ref, acc_ref, *, out_scale):
    j = pl.program_id(2)
    tq = q_ref.shape[2]

    @pl.when(j == 0)
    def _():
        lane = lax.broadcasted_iota(jnp.int32, (tq, LANES), 1)
        first = lane < DA_HEAD_DIM
        for hd in range(DA_HEADS):
            q = q_ref[0, hd]
            w_ref[hd, 0:tq] = jnp.where(first, q, jnp.zeros_like(q))
            w_ref[hd, tq:2 * tq] = jnp.where(first, jnp.zeros_like(q), q)
        m_ref[...] = jnp.full(m_ref.shape, -jnp.inf, _F32)
        acc_ref[...] = jnp.zeros(acc_ref.shape, _F32)

    _flash_items(DA_HEADS, k_ref.shape[2], DA_KEY_SUB,
                 lambda hd, ks: _dot_nt(k_ref[0, hd, ks, :], w_ref[hd]),
                 lambda hd, ks: vt_ref[0, hd, :, ks], m_ref, acc_ref)

    @pl.when(j == pl.num_programs(2) - 1)
    def _():
        lam = lam_ref[0]
        for hd in range(DA_HEADS):
            a = acc_ref[hd]
            o = a[0:DA_V_DIM] / a[DA_V_DIM:DA_V_DIM + 1]
            od = (o[:, 0:tq] - lam * o[:, tq:2 * tq]).T
            o_ref[0, :, hd * LANES:(hd + 1) * LANES] = (_rms(od, sub_ref[...]) * out_scale).astype(o_ref.dtype)


def _da_attention(lam1, q, k, vt, subln, lambda_init):
    b, nh, s, _ = q.shape
    tq, tk = min(QUERY_TILE, s), min(DA_KEY_BLOCK, s)
    return pl.pallas_call(
        functools.partial(_da_kernel, out_scale=1.0 - lambda_init),
        grid=(b, s // tq, s // tk),
        in_specs=[pl.BlockSpec(memory_space=pltpu.SMEM),
                  pl.BlockSpec((1, nh, tq, LANES), lambda bi, i, j: (bi, 0, i, 0)),
                  pl.BlockSpec((1, nh, tk, LANES), lambda bi, i, j: (bi, 0, j, 0)),
                  pl.BlockSpec((1, nh, DA_V_DIM, tk), lambda bi, i, j: (bi, 0, 0, j)),
                  pl.BlockSpec((1, DA_V_DIM), lambda bi, i, j: (0, 0))],
        out_specs=pl.BlockSpec((1, tq, D_MODEL), lambda bi, i, j: (bi, i, 0)),
        out_shape=jax.ShapeDtypeStruct((b, s, D_MODEL), _BF16),
        scratch_shapes=[pltpu.VMEM((nh, 2 * tq, LANES), _BF16),
                        pltpu.VMEM((nh, 1, 2 * tq), _F32),
                        pltpu.VMEM((nh, DA_V_DIM + BF16_SUBLANES, 2 * tq), _F32)],
        compiler_params=_params(("parallel", "parallel", "arbitrary")),
        name="da_attention",
    )(lam1, q, k, vt, subln)


def _mla_kernel(q_ref, k_ref, vt_ref, o_ref, m_ref, acc_ref):
    j = pl.program_id(2)

    @pl.when(j == 0)
    def _():
        m_ref[...] = jnp.full(m_ref.shape, -jnp.inf, _F32)
        acc_ref[...] = jnp.zeros(acc_ref.shape, _F32)

    _flash_items(MLA_HEADS, k_ref.shape[2], MLA_KEY_SUB,
                 lambda hd, ks: _dot_nt(k_ref[0, hd, ks, :], q_ref[0, hd]),
                 lambda hd, ks: vt_ref[0, hd, :, ks], m_ref, acc_ref)

    @pl.when(j == pl.num_programs(2) - 1)
    def _():
        for pair in range(MLA_HEADS // 2):
            outs = []
            for hd in (2 * pair, 2 * pair + 1):
                a = acc_ref[hd]
                outs.append(a[0:MLA_V] / a[MLA_V:MLA_V + 1])
            o_ref[0, :, pair * LANES:(pair + 1) * LANES] = jnp.concatenate(outs, axis=0).T.astype(o_ref.dtype)


def _mla_attention(q, k, vt):
    b, nh, s, _ = q.shape
    tq, tk = min(QUERY_TILE, s), min(MLA_KEY_BLOCK, s)
    return pl.pallas_call(
        _mla_kernel,
        grid=(b, s // tq, s // tk),
        in_specs=[pl.BlockSpec((1, nh, tq, LANES), lambda bi, i, j: (bi, 0, i, 0)),
                  pl.BlockSpec((1, nh, tk, LANES), lambda bi, i, j: (bi, 0, j, 0)),
                  pl.BlockSpec((1, nh, MLA_V, tk), lambda bi, i, j: (bi, 0, 0, j))],
        out_specs=pl.BlockSpec((1, tq, D_MODEL), lambda bi, i, j: (bi, i, 0)),
        out_shape=jax.ShapeDtypeStruct((b, s, D_MODEL), _BF16),
        scratch_shapes=[pltpu.VMEM((nh, 1, tq), _F32),
                        pltpu.VMEM((nh, MLA_V + BF16_SUBLANES, tq), _F32)],
        compiler_params=_params(("parallel", "parallel", "arbitrary")),
        name="mla_attention",
    )(q, k, vt)


def _outp_kernel(x_ref, mod_ref, oa_ref, ob_ref, ga_ref, gb_ref, wba_ref, wbb_ref, wo_ref,
                 npm_ref, npf_ref, wr_ref, br_ref,
                 x1_ref, h2_ref, idx_ref, pos_ref, w_ref, cnt_ref, carry_ref):
    first = jnp.logical_and(pl.program_id(0) == 0, pl.program_id(1) == 0)

    @pl.when(first)
    def _():
        carry_ref[...] = jnp.zeros(carry_ref.shape, _F32)

    mod = mod_ref[0]
    g1 = mod[:, 2 * D_MODEL:3 * D_MODEL]
    sh2 = mod[:, 3 * D_MODEL:4 * D_MODEL]
    sc2 = mod[:, 4 * D_MODEL:5 * D_MODEL]
    merged = (ga_ref[0].astype(_F32) * _dot(oa_ref[0], wba_ref[...])
              + gb_ref[0].astype(_F32) * _dot(ob_ref[0], wbb_ref[...]))
    mix = _dot(merged.astype(_BF16), wo_ref[...])
    x1 = x_ref[0] + g1 * _rms(mix, npm_ref[...])
    x1_ref[0] = x1
    h2 = _rms(x1, npf_ref[...]) * (1.0 + sc2) + sh2
    h2_ref[0] = h2

    logits = jnp.dot(h2, wr_ref[...], preferred_element_type=_F32,
                     precision=lax.Precision.HIGHEST) + br_ref[...]
    tm = logits.shape[0]
    lane = lax.broadcasted_iota(jnp.int32, logits.shape, 1)
    vals = logits
    sels, idxs, exps = [], [], []
    top = None
    for _ in range(TOP_K):
        mx = jnp.max(vals, axis=-1, keepdims=True)
        idx = jnp.min(jnp.where(vals == mx, lane, LANES), axis=-1, keepdims=True)
        sel = lane == idx
        top = mx if top is None else top
        sels.append(sel)
        idxs.append(idx)
        exps.append(jnp.exp(mx - top))
        vals = jnp.where(sel, -jnp.inf, vals)
    den = exps[0] + exps[1] + exps[2] + exps[3]

    onehot = jnp.zeros(logits.shape, _F32)
    for sel in sels:
        onehot = onehot + jnp.where(sel, 1.0, 0.0)
    r_i = lax.broadcasted_iota(jnp.int32, (tm, tm), 0)
    c_i = lax.broadcasted_iota(jnp.int32, (tm, tm), 1)
    earlier = jnp.where(r_i > c_i, 1.0, 0.0).astype(_BF16)
    rank = _dot(earlier, onehot.astype(_BF16)) + carry_ref[0:1, :]
    idx4 = jnp.zeros(logits.shape, jnp.int32)
    pos4 = jnp.zeros(logits.shape, jnp.int32)
    w4 = jnp.zeros(logits.shape, _F32)
    for r in range(TOP_K):
        pos = jnp.sum(jnp.where(sels[r], rank, 0.0), axis=-1, keepdims=True)
        idx4 = jnp.where(lane == r, idxs[r], idx4)
        pos4 = jnp.where(lane == r, pos.astype(jnp.int32), pos4)
        w4 = jnp.where(lane == r, exps[r] / den, w4)
    idx_ref[0] = idx4
    pos_ref[0] = pos4
    w_ref[0] = w4
    carry_ref[...] = carry_ref[...] + jnp.sum(onehot, axis=0, keepdims=True)
    cnt_ref[...] = carry_ref[...]


def _outp(x, mod3, oa, ob, ga, gb, wba, wbb, wo, npm, npf, wr, br):
    b, s, _ = x.shape
    tm = min(256, s)
    tok = pl.BlockSpec((1, tm, D_MODEL), lambda bi, i: (bi, i, 0))
    meta = pl.BlockSpec((1, tm, LANES), lambda bi, i: (bi, i, 0))
    sq = _const_spec((D_MODEL, D_MODEL))
    row = _const_spec((1, D_MODEL))
    return pl.pallas_call(
        _outp_kernel,
        grid=(b, s // tm),
        in_specs=[tok, pl.BlockSpec((1, 1, N_MOD * D_MODEL), lambda bi, i: (bi, 0, 0)),
                  tok, tok, tok, tok, sq, sq, sq, row, row,
                  _const_spec((D_MODEL, LANES)), _const_spec((1, LANES))],
        out_specs=[tok, tok, meta, meta, meta, pl.BlockSpec((8, LANES), lambda bi, i: (0, 0))],
        out_shape=[jax.ShapeDtypeStruct((b, s, D_MODEL), _F32),
                   jax.ShapeDtypeStruct((b, s, D_MODEL), _F32),
                   jax.ShapeDtypeStruct((b, s, LANES), jnp.int32),
                   jax.ShapeDtypeStruct((b, s, LANES), jnp.int32),
                   jax.ShapeDtypeStruct((b, s, LANES), _F32),
                   jax.ShapeDtypeStruct((8, LANES), _F32)],
        scratch_shapes=[pltpu.VMEM((8, LANES), _F32)],
        compiler_params=_params(("arbitrary", "arbitrary")),
        name="outp",
    )(x, mod3, oa, ob, ga, gb, wba, wbb, wo, npm, npf, wr, br)


MOE_ROW_TILE = 512
ROUTE_TILE = 256


def _scatter_kernel(dest_ref, h_ref, init_ref, xs_ref, sem):
    del init_ref
    tm = h_ref.shape[0]

    def issue(t, carry):
        for r in range(TOP_K):
            d = dest_ref[TOP_K * t + r]
            pltpu.make_async_copy(h_ref.at[pl.ds(t, 1)], xs_ref.at[pl.ds(d, 1)], sem).start()
        return carry

    lax.fori_loop(0, tm, issue, 0)
    for _ in range(TOP_K):
        pltpu.make_async_copy(h_ref, xs_ref.at[pl.ds(0, tm)], sem).wait()


def _scatter_rows(dest, h2, n_rows):
    t = h2.shape[0]
    tm = min(ROUTE_TILE, t)
    return pl.pallas_call(
        _scatter_kernel,
        grid=(t // tm,),
        in_specs=[pl.BlockSpec((TOP_K * tm,), lambda i: (i,), memory_space=pltpu.SMEM),
                  pl.BlockSpec((tm, D_MODEL), lambda i: (i, 0)),
                  pl.BlockSpec(memory_space=pl.ANY)],
        out_specs=pl.BlockSpec(memory_space=pl.ANY),
        out_shape=jax.ShapeDtypeStruct((n_rows, D_MODEL), _F32),
        scratch_shapes=[pltpu.SemaphoreType.DMA(())],
        input_output_aliases={2: 0},
        compiler_params=_params(("arbitrary",)),
        name="moe_scatter",
    )(dest, h2, jnp.zeros((n_rows, D_MODEL), _F32))


def _experts_kernel(te_ref, xs_ref, wgu_ref, bgu_ref, wd_ref, bd_ref, ys_ref):
    del te_ref
    gu = _dot(xs_ref[...].astype(_BF16), wgu_ref[0]) + bgu_ref[0]
    glu = jnp.minimum(gu[:, :D_EXPERT], SWIGLU_LIMIT)
    lin = jnp.clip(gu[:, D_EXPERT:], -SWIGLU_LIMIT, SWIGLU_LIMIT)
    act = glu * jax.nn.sigmoid(SWIGLU_ALPHA * glu) * (lin + 1.0)
    ys_ref[...] = _dot(act.astype(_BF16), wd_ref[0]) + bd_ref[0]


def _experts(tile_expert, xs, wgu, bgu, wd, bd):
    n_rows = xs.shape[0]
    tr = MOE_ROW_TILE
    rows = pl.BlockSpec((tr, D_MODEL), lambda g, te: (g, 0))
    return pl.pallas_call(
        _experts_kernel,
        grid_spec=pltpu.PrefetchScalarGridSpec(
            num_scalar_prefetch=1,
            grid=(n_rows // tr,),
            in_specs=[rows,
                      pl.BlockSpec((1, D_MODEL, 2 * D_EXPERT), lambda g, te: (te[g], 0, 0)),
                      pl.BlockSpec((1, 1, 2 * D_EXPERT), lambda g, te: (te[g], 0, 0)),
                      pl.BlockSpec((1, D_EXPERT, D_MODEL), lambda g, te: (te[g], 0, 0)),
                      pl.BlockSpec((1, 1, D_MODEL), lambda g, te: (te[g], 0, 0))],
            out_specs=rows),
        out_shape=jax.ShapeDtypeStruct((n_rows, D_MODEL), _F32),
        compiler_params=_params(("arbitrary",)),
        name="moe_experts",
    )(tile_expert, xs, wgu, bgu, wd, bd)


def _combine_kernel(dest_ref, ys_ref, w_ref, x1_ref, mod_ref, npo_ref, o_ref, buf_ref, sem):
    tm = x1_ref.shape[1]

    def issue(t, carry):
        for r in range(TOP_K):
            d = dest_ref[TOP_K * t + r]
            pltpu.make_async_copy(ys_ref.at[pl.ds(d, 1)], buf_ref.at[r, pl.ds(t, 1)], sem).start()
        return carry

    lax.fori_loop(0, tm, issue, 0)
    for r in range(TOP_K):
        pltpu.make_async_copy(ys_ref.at[pl.ds(0, tm)], buf_ref.at[r], sem).wait()
    w = w_ref[0]
    ff = w[:, 0:1] * buf_ref[0]
    for r in range(1, TOP_K):
        ff = ff + w[:, r:r + 1] * buf_ref[r]
    g2 = mod_ref[0][:, 5 * D_MODEL:6 * D_MODEL]
    o_ref[0] = x1_ref[0] + g2 * _rms(ff, npo_ref[...])


def _combine(dest, ys, w4, x1, mod3, npo):
    b, s, _ = x1.shape
    tm = min(ROUTE_TILE, s)
    nt = s // tm
    tok = pl.BlockSpec((1, tm, D_MODEL), lambda bi, i: (bi, i, 0))
    return pl.pallas_call(
        _combine_kernel,
        grid=(b, nt),
        in_specs=[pl.BlockSpec((TOP_K * tm,), lambda bi, i: (bi * nt + i,), memory_space=pltpu.SMEM),
                  pl.BlockSpec(memory_space=pl.ANY),
                  pl.BlockSpec((1, tm, LANES), lambda bi, i: (bi, i, 0)),
                  tok, pl.BlockSpec((1, 1, N_MOD * D_MODEL), lambda bi, i: (bi, 0, 0)),
                  pl.BlockSpec((1, D_MODEL), lambda bi, i: (0, 0))],
        out_specs=tok,
        out_shape=jax.ShapeDtypeStruct((b, s, D_MODEL), _F32),
        scratch_shapes=[pltpu.VMEM((TOP_K, tm, D_MODEL), _F32), pltpu.SemaphoreType.DMA(())],
        compiler_params=_params(("arbitrary", "arbitrary")),
        name="moe_combine",
    )(dest, ys, w4, x1, mod3, npo)


def _moe(h2, idx4, pos4, w4, counts, wgu, bgu, wd, bd, x1, mod3, npo):
    b, s, _ = h2.shape
    t = b * s
    tr = MOE_ROW_TILE
    n_tiles = (t * TOP_K) // tr + N_EXPERTS
    cnt = counts[0, :N_EXPERTS].astype(jnp.int32)
    padded = ((cnt + tr - 1) // tr) * tr
    ends = jnp.cumsum(padded)
    starts = ends - padded
    sel = idx4[..., :TOP_K, None] == jnp.arange(N_EXPERTS, dtype=jnp.int32)
    dest = (jnp.sum(jnp.where(sel, starts, 0), axis=-1) + pos4[..., :TOP_K]).reshape(t * TOP_K)
    tile_row0 = jnp.arange(n_tiles, dtype=jnp.int32) * tr
    tile_expert = jnp.sum((ends[None, :] <= tile_row0[:, None]).astype(jnp.int32), axis=-1)
    tile_expert = jnp.minimum(tile_expert, N_EXPERTS - 1)
    xs = _scatter_rows(dest, h2.reshape(t, D_MODEL), n_tiles * tr)
    ys = _experts(tile_expert, xs, wgu, bgu, wd, bd)
    return _combine(dest, ys, w4, x1, mod3, npo)


def _rope_tables(seq, dim):
    inv = 1.0 / (ROPE_THETA ** (jnp.arange(0, dim, 2, dtype=_F32) / dim))
    ang = jnp.arange(seq, dtype=_F32)[:, None] * inv[None, :]
    ang = jnp.concatenate([ang, ang], axis=-1)
    return jnp.cos(ang), jnp.sin(ang)


def _tables(seq):
    cos_a, sin_a = _rope_tables(seq, DA_HEAD_DIM)
    half = DA_HEAD_DIM // 2
    sin_a = jnp.concatenate([-sin_a[:, :half], sin_a[:, half:]], axis=-1)
    cosa = jnp.tile(cos_a, (1, LANES // DA_HEAD_DIM))
    sina = jnp.tile(sin_a, (1, LANES // DA_HEAD_DIM))
    cos_b, sin_b = _rope_tables(seq, MLA_ROPE)
    mh = MLA_ROPE // 2
    sin_b = jnp.concatenate([-sin_b[:, :mh], sin_b[:, mh:]], axis=-1)
    pad = LANES - MLA_NOPE - MLA_ROPE
    cosm = jnp.concatenate([jnp.ones((seq, MLA_NOPE), _F32), cos_b, jnp.ones((seq, pad), _F32)], axis=-1)
    sinm = jnp.concatenate([jnp.zeros((seq, MLA_NOPE), _F32), sin_b, jnp.zeros((seq, pad), _F32)], axis=-1)
    return cosa, sina, cosm, sinm


def _pack_weights(w_in, mla_w_uq, mla_w_ukv):
    o = [0, 1024, 2048, 3072, 3456, 3712, 3744, 4768, 5792]
    qkv, cq, ckv, kr, gates = w_in[:, o[0]:o[3]], w_in[:, o[3]:o[4]], w_in[:, o[4]:o[5]], w_in[:, o[5]:o[6]], w_in[:, o[6]:o[8]]
    kr128 = jnp.pad(kr, ((0, 0), (MLA_NOPE, LANES - MLA_NOPE - MLA_ROPE)))
    wp = jnp.concatenate([qkv, gates, cq, ckv, kr128], axis=1).astype(_BF16)
    qk_dim = MLA_NOPE + MLA_ROPE
    wuq = mla_w_uq.reshape(MLA_Q_RANK, MLA_HEADS, qk_dim)
    wuq = jnp.pad(wuq, ((0, 0), (0, 0), (0, LANES - qk_dim))).reshape(MLA_Q_RANK, MLA_HEADS * LANES).astype(_BF16)
    wukv = mla_w_ukv.reshape(MLA_KV_RANK, MLA_HEADS, MLA_NOPE + MLA_V)
    wuk = jnp.pad(wukv[:, :, :MLA_NOPE], ((0, 0), (0, 0), (0, LANES - MLA_NOPE)))
    wuk = wuk.reshape(MLA_KV_RANK, MLA_HEADS * LANES).astype(_BF16)
    wuv = wukv[:, :, MLA_NOPE:].reshape(MLA_KV_RANK, MLA_HEADS * MLA_V).astype(_BF16)
    return wp, wuq, wuk, wuv


def _layer(x, mod, lam1, lambda_init, p):
    b, s, _ = x.shape
    mod3 = mod.reshape(b, 1, N_MOD * D_MODEL)
    cosa, sina, cosm, sinm = _tables(s)
    qda, kda, vtda, qm, km, vtm, ga, gb = _inproj(
        x, mod3, p["n_pre_mix"], p["wp"], p["q_norm"], p["kv_norm"], p["wuq"], p["wuk"], p["wuv"],
        cosa, sina, cosm, sinm)
    oa = _da_attention(lam1, qda, kda, vtda, p["da_subln"], lambda_init)
    ob = _mla_attention(qm, km, vtm)
    x1, h2, idx4, pos4, w4, counts = _outp(x, mod3, oa, ob, ga, gb, p["wba"], p["wbb"], p["wo"],
                                           p["n_post_mix"], p["n_pre_ffn"], p["wr"], p["br"])
    return _moe(h2, idx4, pos4, w4, counts, p["wgu"], p["bgu"], p["wd"], p["bd"], x1, mod3, p["n_post_ffn"])


def kernel(x_prompt, x_sample, c_prompt, c_sample, w_ada, b_ada, norm_pre_mix, norm_post_mix, norm_pre_ffn, norm_post_ffn, w_in, da_lambda_q1, da_lambda_k1, da_lambda_q2, da_lambda_k2, da_subln, mla_q_norm, mla_kv_norm, mla_w_uq, mla_w_ukv, w_branch_a, w_branch_b, w_out, w_router, b_router, w_gate_up, b_gate_up, w_down, b_down):
    depth = w_ada.shape[0]
    xp, xs = x_prompt, x_sample
    bp, bs = c_prompt.shape[0], c_sample.shape[0]
    rows = -(-(bp + bs) // 8) * 8
    c_all = jnp.concatenate([c_prompt, c_sample, jnp.zeros((rows - bp - bs, D_MODEL), _F32)], axis=0)
    row = lambda v: v.reshape(1, -1)
    for l in range(depth):
        lambda_init = 0.8 - 0.6 * math.exp(-0.3 * l)
        mod, lam = _ada(c_all, w_ada[l], b_ada[l], da_lambda_q1[l], da_lambda_k1[l],
                        da_lambda_q2[l], da_lambda_k2[l], lambda_init)
        lam1 = lam[0, 0:1]
        wp, wuq, wuk, wuv = _pack_weights(w_in[l], mla_w_uq[l], mla_w_ukv[l])
        p = dict(
            n_pre_mix=row(norm_pre_mix[l]), n_post_mix=row(norm_post_mix[l]),
            n_pre_ffn=row(norm_pre_ffn[l]), n_post_ffn=row(norm_post_ffn[l]),
            wp=wp, wuq=wuq, wuk=wuk, wuv=wuv,
            q_norm=row(mla_q_norm[l]), kv_norm=row(mla_kv_norm[l]), da_subln=row(da_subln[l]),
            wba=w_branch_a[l].astype(_BF16), wbb=w_branch_b[l].astype(_BF16), wo=w_out[l].astype(_BF16),
            wr=jnp.pad(w_router[l], ((0, 0), (0, LANES - N_EXPERTS))),
            br=jnp.pad(row(b_router[l]), ((0, 0), (0, LANES - N_EXPERTS)), constant_values=-1e30),
            wgu=w_gate_up[l].astype(_BF16), bgu=b_gate_up[l].reshape(N_EXPERTS, 1, 2 * D_EXPERT),
            wd=w_down[l].astype(_BF16), bd=b_down[l].reshape(N_EXPERTS, 1, D_MODEL),
        )
        xp = _layer(xp, mod[0:bp], lam1, lambda_init, p)
        xs = _layer(xs, mod[bp:bp + bs], lam1, lambda_init, p)
    return (xp, xs)
```

```python
import functools
import math

import jax
import jax.numpy as jnp
from jax import lax
from jax.experimental import pallas as pl
from jax.experimental.pallas import tpu as pltpu

D_MODEL = 1024
DA_HEADS = 8
DA_HEAD_DIM = 64
DA_V_DIM = 128
MLA_HEADS = 16
MLA_NOPE = 64
MLA_ROPE = 32
MLA_V = 64
MLA_Q_RANK = 384
MLA_KV_RANK = 256
N_EXPERTS = 32
TOP_K = 4
D_EXPERT = 1024
SWIGLU_LIMIT = 7.0
SWIGLU_ALPHA = 1.702
ROPE_THETA = 10000.0
EPS = 1e-6
N_MOD = 6

LANES = 128
BF16_SUBLANES = 16
VMEM_LIMIT = 56 * 1024 * 1024
LOG2E = 1.4426950408889634

_QKV0, _GATE0, _CQ0, _CKV0, _KR0, _WP = 0, 3072, 5120, 5504, 5760, 5888

_F32 = jnp.float32
_BF16 = jnp.bfloat16
_NT = (((1,), (1,)), ((), ()))


def _dot(a, b):
    return jnp.dot(a, b, preferred_element_type=_F32)


def _dot_nt(a, b):
    return lax.dot_general(a, b, _NT, preferred_element_type=_F32)


def _bf16_truncate(x):
    bits = lax.bitcast_convert_type(x, jnp.uint32) & jnp.uint32(0xFFFF0000)
    return lax.bitcast_convert_type(bits, _F32)


def _rms(x, w):
    return x * lax.rsqrt(jnp.mean(x * x, axis=-1, keepdims=True) + EPS) * w


def _const_spec(shape):
    nd = len(shape)
    return pl.BlockSpec(shape, lambda *_: (0,) * nd, pipeline_mode=pl.Buffered(1))


def _params(sem):
    return pltpu.CompilerParams(dimension_semantics=sem, vmem_limit_bytes=VMEM_LIMIT)


def _ada_kernel(c_ref, w_ref, b_ref, lq1_ref, lk1_ref, lq2_ref, lk2_ref, mod_ref, lam_ref, *, lambda_init):
    c = c_ref[...]
    s = c * jax.nn.sigmoid(c)
    mod_ref[...] = jnp.dot(s, w_ref[...], preferred_element_type=_F32,
                           precision=lax.Precision.HIGHEST) + b_ref[...]
    d1 = jnp.sum(lq1_ref[...] * lk1_ref[...], axis=-1, keepdims=True)
    d2 = jnp.sum(lq2_ref[...] * lk2_ref[...], axis=-1, keepdims=True)
    lam = jnp.exp(d1) - jnp.exp(d2) + lambda_init
    lam_ref[...] = jnp.broadcast_to(lam, lam_ref.shape)


def _ada(c_all, w_ada, b_ada, lq1, lk1, lq2, lk2, lambda_init):
    rows = c_all.shape[0]
    ncol = w_ada.shape[1]
    tn = D_MODEL
    vec = lambda r: r.reshape(1, -1)
    small = pl.BlockSpec((1, DA_HEAD_DIM), lambda j: (0, 0))
    return pl.pallas_call(
        functools.partial(_ada_kernel, lambda_init=lambda_init),
        grid=(ncol // tn,),
        in_specs=[pl.BlockSpec((rows, D_MODEL), lambda j: (0, 0)),
                  pl.BlockSpec((D_MODEL, tn), lambda j: (0, j)),
                  pl.BlockSpec((1, tn), lambda j: (0, j)),
                  small, small, small, small],
        out_specs=[pl.BlockSpec((rows, tn), lambda j: (0, j)),
                   pl.BlockSpec((8, LANES), lambda j: (0, 0))],
        out_shape=[jax.ShapeDtypeStruct((rows, ncol), _F32),
                   jax.ShapeDtypeStruct((8, LANES), _F32)],
        compiler_params=_params(("arbitrary",)),
        name="ada",
    )(c_all, w_ada, vec(b_ada), vec(lq1), vec(lk1), vec(lq2), vec(lk2))


def _rope128(x, cos, sin_signed, first_half, shift_lo, shift_hi):
    r = jnp.where(first_half, pltpu.roll(x, shift_lo, 1), pltpu.roll(x, shift_hi, 1))
    return x * cos + r * sin_signed


def _inproj_kernel(x_ref, mod_ref, n1_ref, wp_ref, qn_ref, kvn_ref, wuq_ref, wuk_ref, wuv_ref,
                   cosa_ref, sina_ref, cosm_ref, sinm_ref,
                   qda_ref, kda_ref, vtda_ref, qm_ref, km_ref, vtm_ref, ga_ref, gb_ref,
                   *, da_scale, mla_scale):
    x = x_ref[0]
    mod = mod_ref[0]
    sh1 = mod[:, 0:D_MODEL]
    sc1 = mod[:, D_MODEL:2 * D_MODEL]
    h = _rms(x, n1_ref[...]) * (1.0 + sc1) + sh1
    hb = h.astype(_BF16)

    lane = lax.broadcasted_iota(jnp.int32, (x.shape[0], LANES), 1)
    da_first = (lane % DA_HEAD_DIM) < (DA_HEAD_DIM // 2)
    pe_mid = MLA_NOPE + MLA_ROPE // 2
    mla_first = lane < pe_mid
    cosa, sina = cosa_ref[...], sina_ref[...]
    cosm, sinm = cosm_ref[...], sinm_ref[...]
    half = DA_HEAD_DIM // 2
    mhalf = MLA_ROPE // 2

    for pair in range(DA_HEADS // 2):
        c0 = 2 * pair * LANES
        zq = _dot(hb, wp_ref[:, _QKV0 + c0:_QKV0 + c0 + 2 * LANES])
        zk = _dot(hb, wp_ref[:, _QKV0 + D_MODEL + c0:_QKV0 + D_MODEL + c0 + 2 * LANES])
        zv = _dot(hb, wp_ref[:, _QKV0 + 2 * D_MODEL + c0:_QKV0 + 2 * D_MODEL + c0 + 2 * LANES])
        for t in range(2):
            hd = 2 * pair + t
            sl = slice(t * LANES, (t + 1) * LANES)
            q = _rope128(zq[:, sl], cosa, sina, da_first, LANES - half, half)
            qda_ref[0, hd] = (q * da_scale).astype(_BF16)
            k = _rope128(zk[:, sl], cosa, sina, da_first, LANES - half, half)
            kda_ref[0, hd] = k.astype(_BF16)
            vtda_ref[0, hd] = zv[:, sl].T.astype(_BF16)

    for blk in range(2 * D_MODEL // 512):
        z = _dot(hb, wp_ref[:, _GATE0 + blk * 512:_GATE0 + (blk + 1) * 512])
        g = jax.nn.sigmoid(z).astype(_BF16)
        if blk < D_MODEL // 512:
            ga_ref[0, :, blk * 512:(blk + 1) * 512] = g
        else:
            o = blk * 512 - D_MODEL
            gb_ref[0, :, o:o + 512] = g

    cq = _dot(hb, wp_ref[:, _CQ0:_CQ0 + MLA_Q_RANK])
    cqn = _rms(cq, qn_ref[...]).astype(_BF16)
    for pair in range(MLA_HEADS // 2):
        z = _dot(cqn, wuq_ref[:, 2 * pair * LANES:(2 * pair + 2) * LANES])
        for t in range(2):
            q = _rope128(z[:, t * LANES:(t + 1) * LANES], cosm, sinm, mla_first, LANES - mhalf, mhalf)
            qm_ref[0, 2 * pair + t] = (q * mla_scale).astype(_BF16)

    ckv = _dot(hb, wp_ref[:, _CKV0:_CKV0 + MLA_KV_RANK])
    ckvn = _rms(ckv, kvn_ref[...]).astype(_BF16)
    kr = _dot(hb, wp_ref[:, _KR0:_KR0 + LANES])
    kr = _rope128(kr, cosm, sinm, mla_first, LANES - mhalf, mhalf)
    for pair in range(MLA_HEADS // 2):
        zk = _dot(ckvn, wuk_ref[:, 2 * pair * LANES:(2 * pair + 2) * LANES])
        for t in range(2):
            km_ref[0, 2 * pair + t] = (zk[:, t * LANES:(t + 1) * LANES] + kr).astype(_BF16)
        vt = _dot(ckvn, wuv_ref[:, pair * LANES:(pair + 1) * LANES]).T.astype(_BF16)
        vtm_ref[0, 2 * pair] = vt[0:MLA_V]
        vtm_ref[0, 2 * pair + 1] = vt[MLA_V:2 * MLA_V]


def _inproj(x, mod3, n1, wp, qn, kvn, wuq, wuk, wuv, cosa, sina, cosm, sinm):
    b, s, _ = x.shape
    tm = min(512, s)
    bf = lambda *shape: jax.ShapeDtypeStruct(shape, _BF16)
    tok = pl.BlockSpec((1, tm, D_MODEL), lambda bi, i: (bi, i, 0))
    tab = pl.BlockSpec((tm, LANES), lambda bi, i: (i, 0))
    hrow = lambda nh: pl.BlockSpec((1, nh, tm, LANES), lambda bi, i: (bi, 0, i, 0))
    hcol = lambda nh, r: pl.BlockSpec((1, nh, r, tm), lambda bi, i: (bi, 0, 0, i))
    kern = functools.partial(_inproj_kernel,
                             da_scale=DA_HEAD_DIM ** -0.5 * LOG2E,
                             mla_scale=(MLA_NOPE + MLA_ROPE) ** -0.5 * LOG2E)
    return pl.pallas_call(
        kern,
        grid=(b, s // tm),
        in_specs=[tok,
                  pl.BlockSpec((1, 1, N_MOD * D_MODEL), lambda bi, i: (bi, 0, 0)),
                  _const_spec((1, D_MODEL)),
                  _const_spec((D_MODEL, _WP)),
                  _const_spec((1, MLA_Q_RANK)),
                  _const_spec((1, MLA_KV_RANK)),
                  _const_spec((MLA_Q_RANK, MLA_HEADS * LANES)),
                  _const_spec((MLA_KV_RANK, MLA_HEADS * LANES)),
                  _const_spec((MLA_KV_RANK, MLA_HEADS * MLA_V)),
                  tab, tab, tab, tab],
        out_specs=[hrow(DA_HEADS), hrow(DA_HEADS), hcol(DA_HEADS, DA_V_DIM),
                   hrow(MLA_HEADS), hrow(MLA_HEADS), hcol(MLA_HEADS, MLA_V), tok, tok],
        out_shape=[bf(b, DA_HEADS, s, LANES), bf(b, DA_HEADS, s, LANES), bf(b, DA_HEADS, DA_V_DIM, s),
                   bf(b, MLA_HEADS, s, LANES), bf(b, MLA_HEADS, s, LANES), bf(b, MLA_HEADS, MLA_V, s),
                   bf(b, s, D_MODEL), bf(b, s, D_MODEL)],
        compiler_params=_params(("parallel", "parallel")),
        name="inproj",
    )(x, mod3, n1, wp, qn, kvn, wuq, wuk, wuv, cosa, sina, cosm, sinm)


DA_KEY_SUB, DA_KEY_BLOCK = 512, 2048
MLA_KEY_SUB, MLA_KEY_BLOCK = 1024, 2048
QUERY_TILE = 512


def _flash_items(n_heads, tk, key_sub, scores, vt_of, m_ref, acc_ref):
    sub = min(key_sub, tk)
    items = [(hd, pl.ds(c * sub, sub)) for c in range(tk // sub) for hd in range(n_heads)]

    def stage_a(hd, ks):
        st = scores(hd, ks)
        m_old = m_ref[hd]
        m_new = jnp.maximum(m_old, jnp.max(st, axis=0, keepdims=True))
        m_ref[hd] = m_new
        return st, m_new, jnp.exp2(m_old - m_new)

    def stage_c(hd, ks, p, alpha):
        vt = vt_of(hd, ks)
        ones = jnp.ones((BF16_SUBLANES, vt.shape[1]), _BF16)
        pv = _dot(jnp.concatenate([vt, ones], axis=0), p)
        acc_ref[hd] = alpha * acc_ref[hd] + pv

    a_out, b_out = {}, {}
    for t in range(len(items) + 2):
        if t < len(items):
            a_out[t] = stage_a(*items[t])
        if 0 <= t - 1 < len(items):
            st, m_new, alpha = a_out.pop(t - 1)
            b_out[t - 1] = (jnp.exp2(st - m_new).astype(_BF16), alpha)
        if 0 <= t - 2 < len(items):
            stage_c(*items[t - 2], *b_out.pop(t - 2))


def _da_kernel(lam_ref, q_ref, k_ref, vt_ref, sub_ref, o_ref, w_ref, m_ref, acc_ref, *, out_scale):
    j = pl.program_id(2)
    tq = q_ref.shape[2]

    @pl.when(j == 0)
    def _():
        lane = lax.broadcasted_iota(jnp.int32, (tq, LANES), 1)
        first = lane < DA_HEAD_DIM
        for hd in range(DA_HEADS):
            q = q_ref[0, hd]
            w_ref[hd, 0:tq] = jnp.where(first, q, jnp.zeros_like(q))
            w_ref[hd, tq:2 * tq] = jnp.where(first, jnp.zeros_like(q), q)
        m_ref[...] = jnp.full(m_ref.shape, -jnp.inf, _F32)
        acc_ref[...] = jnp.zeros(acc_ref.shape, _F32)

    _flash_items(DA_HEADS, k_ref.shape[2], DA_KEY_SUB,
                 lambda hd, ks: _dot_nt(k_ref[0, hd, ks, :], w_ref[hd]),
                 lambda hd, ks: vt_ref[0, hd, :, ks], m_ref, acc_ref)

    @pl.when(j == pl.num_programs(2) - 1)
    def _():
        lam = lam_ref[0]
        for hd in range(DA_HEADS):
            a = acc_ref[hd]
            o = a[0:DA_V_DIM] / a[DA_V_DIM:DA_V_DIM + 1]
            od = (o[:, 0:tq] - lam * o[:, tq:2 * tq]).T
            o_ref[0, :, hd * LANES:(hd + 1) * LANES] = (_rms(od, sub_ref[...]) * out_scale).astype(o_ref.dtype)


def _da_attention(lam1, q, k, vt, subln, lambda_init):
    b, nh, s, _ = q.shape
    tq, tk = min(QUERY_TILE, s), min(DA_KEY_BLOCK, s)
    return pl.pallas_call(
        functools.partial(_da_kernel, out_scale=1.0 - lambda_init),
        grid=(b, s // tq, s // tk),
        in_specs=[pl.BlockSpec(memory_space=pltpu.SMEM),
                  pl.BlockSpec((1, nh, tq, LANES), lambda bi, i, j: (bi, 0, i, 0)),
                  pl.BlockSpec((1, nh, tk, LANES), lambda bi, i, j: (bi, 0, j, 0)),
                  pl.BlockSpec((1, nh, DA_V_DIM, tk), lambda bi, i, j: (bi, 0, 0, j)),
                  pl.BlockSpec((1, DA_V_DIM), lambda bi, i, j: (0, 0))],
        out_specs=pl.BlockSpec((1, tq, D_MODEL), lambda bi, i, j: (bi, i, 0)),
        out_shape=jax.ShapeDtypeStruct((b, s, D_MODEL), _BF16),
        scratch_shapes=[pltpu.VMEM((nh, 2 * tq, LANES), _BF16),
                        pltpu.VMEM((nh, 1, 2 * tq), _F32),
                        pltpu.VMEM((nh, DA_V_DIM + BF16_SUBLANES, 2 * tq), _F32)],
        compiler_params=_params(("parallel", "parallel", "arbitrary")),
        name="da_attention",
    )(lam1, q, k, vt, subln)


def _mla_kernel(q_ref, k_ref, vt_ref, o_ref, m_ref, acc_ref):
    j = pl.program_id(2)

    @pl.when(j == 0)
    def _():
        m_ref[...] = jnp.full(m_ref.shape, -jnp.inf, _F32)
        acc_ref[...] = jnp.zeros(acc_ref.shape, _F32)

    _flash_items(MLA_HEADS, k_ref.shape[2], MLA_KEY_SUB,
                 lambda hd, ks: _dot_nt(k_ref[0, hd, ks, :], q_ref[0, hd]),
                 lambda hd, ks: vt_ref[0, hd, :, ks], m_ref, acc_ref)

    @pl.when(j == pl.num_programs(2) - 1)
    def _():
        for pair in range(MLA_HEADS // 2):
            outs = []
            for hd in (2 * pair, 2 * pair + 1):
                a = acc_ref[hd]
                outs.append(a[0:MLA_V] / a[MLA_V:MLA_V + 1])
            o_ref[0, :, pair * LANES:(pair + 1) * LANES] = jnp.concatenate(outs, axis=0).T.astype(o_ref.dtype)


def _mla_attention(q, k, vt):
    b, nh, s, _ = q.shape
    tq, tk = min(QUERY_TILE, s), min(MLA_KEY_BLOCK, s)
    return pl.pallas_call(
        _mla_kernel,
        grid=(b, s // tq, s // tk),
        in_specs=[pl.BlockSpec((1, nh, tq, LANES), lambda bi, i, j: (bi, 0, i, 0)),
                  pl.BlockSpec((1, nh, tk, LANES), lambda bi, i, j: (bi, 0, j, 0)),
                  pl.BlockSpec((1, nh, MLA_V, tk), lambda bi, i, j: (bi, 0, 0, j))],
        out_specs=pl.BlockSpec((1, tq, D_MODEL), lambda bi, i, j: (bi, i, 0)),
        out_shape=jax.ShapeDtypeStruct((b, s, D_MODEL), _BF16),
        scratch_shapes=[pltpu.VMEM((nh, 1, tq), _F32),
                        pltpu.VMEM((nh, MLA_V + BF16_SUBLANES, tq), _F32)],
        compiler_params=_params(("parallel", "parallel", "arbitrary")),
        name="mla_attention",
    )(q, k, vt)


def _outp_kernel(x_ref, mod_ref, oa_ref, ob_ref, ga_ref, gb_ref, wba_ref, wbb_ref, wo_ref,
                 npm_ref, npf_ref, wr_ref, br_ref,
                 x1_ref, h2_ref, idx_ref, pos_ref, w_ref, cnt_ref, carry_ref):
    first = jnp.logical_and(pl.program_id(0) == 0, pl.program_id(1) == 0)

    @pl.when(first)
    def _():
        carry_ref[...] = jnp.zeros(carry_ref.shape, _F32)

    mod = mod_ref[0]
    g1 = mod[:, 2 * D_MODEL:3 * D_MODEL]
    sh2 = mod[:, 3 * D_MODEL:4 * D_MODEL]
    sc2 = mod[:, 4 * D_MODEL:5 * D_MODEL]
    merged = (ga_ref[0].astype(_F32) * _dot(oa_ref[0], wba_ref[...])
              + gb_ref[0].astype(_F32) * _dot(ob_ref[0], wbb_ref[...]))
    mix = _dot(merged.astype(_BF16), wo_ref[...])
    x1 = x_ref[0] + g1 * _rms(mix, npm_ref[...])
    x1_ref[0] = x1
    h2 = _rms(x1, npf_ref[...]) * (1.0 + sc2) + sh2
    h2_ref[0] = h2

    h_hi = _bf16_truncate(h2)
    h_lo = (h2 - h_hi).astype(_BF16)
    h_hi = h_hi.astype(_BF16)
    logits = _dot(h_hi, wr_ref[0]) + (_dot(h_lo, wr_ref[0]) + _dot(h_hi, wr_ref[1])) + br_ref[...]
    tm = logits.shape[0]
    lane = lax.broadcasted_iota(jnp.int32, logits.shape, 1)
    vals = logits
    sels, idxs, exps = [], [], []
    top = None
    for _ in range(TOP_K):
        mx = jnp.max(vals, axis=-1, keepdims=True)
        idx = jnp.min(jnp.where(vals == mx, lane, LANES), axis=-1, keepdims=True)
        sel = lane == idx
        top = mx if top is None else top
        sels.append(sel)
        idxs.append(idx)
        exps.append(jnp.exp(mx - top))
        vals = jnp.where(sel, -jnp.inf, vals)
    den = exps[0] + exps[1] + exps[2] + exps[3]

    onehot = jnp.zeros(logits.shape, _F32)
    for sel in sels:
        onehot = onehot + jnp.where(sel, 1.0, 0.0)
    r_i = lax.broadcasted_iota(jnp.int32, (tm, tm), 0)
    c_i = lax.broadcasted_iota(jnp.int32, (tm, tm), 1)
    earlier = jnp.where(r_i > c_i, 1.0, 0.0).astype(_BF16)
    rank = _dot(earlier, onehot.astype(_BF16)) + carry_ref[0:1, :]
    idx4 = jnp.zeros(logits.shape, jnp.int32)
    pos4 = jnp.zeros(logits.shape, jnp.int32)
    w4 = jnp.zeros(logits.shape, _F32)
    for r in range(TOP_K):
        pos = jnp.sum(jnp.where(sels[r], rank, 0.0), axis=-1, keepdims=True)
        idx4 = jnp.where(lane == r, idxs[r], idx4)
        pos4 = jnp.where(lane == r, pos.astype(jnp.int32), pos4)
        w4 = jnp.where(lane == r, exps[r] / den, w4)
    idx_ref[0] = idx4
    pos_ref[0] = pos4
    w_ref[0] = w4
    carry_ref[...] = carry_ref[...] + jnp.sum(onehot, axis=0, keepdims=True)
    cnt_ref[...] = carry_ref[...]


def _outp(x, mod3, oa, ob, ga, gb, wba, wbb, wo, npm, npf, wr, br):
    b, s, _ = x.shape
    tm = min(512, s)
    tok = pl.BlockSpec((1, tm, D_MODEL), lambda bi, i: (bi, i, 0))
    meta = pl.BlockSpec((1, tm, LANES), lambda bi, i: (bi, i, 0))
    sq = _const_spec((D_MODEL, D_MODEL))
    row = _const_spec((1, D_MODEL))
    return pl.pallas_call(
        _outp_kernel,
        grid=(b, s // tm),
        in_specs=[tok, pl.BlockSpec((1, 1, N_MOD * D_MODEL), lambda bi, i: (bi, 0, 0)),
                  tok, tok, tok, tok, sq, sq, sq, row, row,
                  _const_spec((2, D_MODEL, LANES)), _const_spec((1, LANES))],
        out_specs=[tok, tok, meta, meta, meta, pl.BlockSpec((8, LANES), lambda bi, i: (0, 0))],
        out_shape=[jax.ShapeDtypeStruct((b, s, D_MODEL), _F32),
                   jax.ShapeDtypeStruct((b, s, D_MODEL), _F32),
                   jax.ShapeDtypeStruct((b, s, LANES), jnp.int32),
                   jax.ShapeDtypeStruct((b, s, LANES), jnp.int32),
                   jax.ShapeDtypeStruct((b, s, LANES), _F32),
                   jax.ShapeDtypeStruct((8, LANES), _F32)],
        scratch_shapes=[pltpu.VMEM((8, LANES), _F32)],
        compiler_params=_params(("arbitrary", "arbitrary")),
        name="outp",
    )(x, mod3, oa, ob, ga, gb, wba, wbb, wo, npm, npf, wr, br)


MOE_ROW_TILE = 512
ROUTE_TILE = 256


def _scatter_kernel(dest_ref, h_ref, init_ref, xs_ref, sem):
    del init_ref
    tm = h_ref.shape[0]

    def issue(t, carry):
        for r in range(TOP_K):
            d = dest_ref[TOP_K * t + r]
            pltpu.make_async_copy(h_ref.at[pl.ds(t, 1)], xs_ref.at[pl.ds(d, 1)], sem).start(priority=r % 2)
        return carry

    lax.fori_loop(0, tm, issue, 0)
    for _ in range(TOP_K):
        pltpu.make_async_copy(h_ref, xs_ref.at[pl.ds(0, tm)], sem).wait()


def _scatter_rows(dest, h2, n_rows):
    t = h2.shape[0]
    tm = min(ROUTE_TILE, t)
    return pl.pallas_call(
        _scatter_kernel,
        grid=(t // tm,),
        in_specs=[pl.BlockSpec((TOP_K * tm,), lambda i: (i,), memory_space=pltpu.SMEM),
                  pl.BlockSpec((tm, D_MODEL), lambda i: (i, 0)),
                  pl.BlockSpec(memory_space=pl.ANY)],
        out_specs=pl.BlockSpec(memory_space=pl.ANY),
        out_shape=jax.ShapeDtypeStruct((n_rows, D_MODEL), _F32),
        scratch_shapes=[pltpu.SemaphoreType.DMA(())],
        input_output_aliases={2: 0},
        compiler_params=_params(("arbitrary",)),
        name="moe_scatter",
    )(dest, h2, jnp.zeros((n_rows, D_MODEL), _F32))


def _experts_kernel(te_ref, xs_ref, wgu_ref, bgu_ref, wd_ref, bd_ref, ys_ref):
    del te_ref
    gu = _dot(xs_ref[...].astype(_BF16), wgu_ref[0]) + bgu_ref[0]
    glu = jnp.minimum(gu[:, :D_EXPERT], SWIGLU_LIMIT)
    lin = jnp.clip(gu[:, D_EXPERT:], -SWIGLU_LIMIT, SWIGLU_LIMIT)
    act = glu * jax.nn.sigmoid(SWIGLU_ALPHA * glu) * (lin + 1.0)
    ys_ref[...] = _dot(act.astype(_BF16), wd_ref[0]) + bd_ref[0]


def _experts(tile_expert, xs, wgu, bgu, wd, bd):
    n_rows = xs.shape[0]
    tr = MOE_ROW_TILE
    rows = pl.BlockSpec((tr, D_MODEL), lambda g, te: (g, 0))
    return pl.pallas_call(
        _experts_kernel,
        grid_spec=pltpu.PrefetchScalarGridSpec(
            num_scalar_prefetch=1,
            grid=(n_rows // tr,),
            in_specs=[rows,
                      pl.BlockSpec((1, D_MODEL, 2 * D_EXPERT), lambda g, te: (te[g], 0, 0)),
                      pl.BlockSpec((1, 1, 2 * D_EXPERT), lambda g, te: (te[g], 0, 0)),
                      pl.BlockSpec((1, D_EXPERT, D_MODEL), lambda g, te: (te[g], 0, 0)),
                      pl.BlockSpec((1, 1, D_MODEL), lambda g, te: (te[g], 0, 0))],
            out_specs=rows),
        out_shape=jax.ShapeDtypeStruct((n_rows, D_MODEL), _F32),
        compiler_params=_params(("arbitrary",)),
        name="moe_experts",
    )(tile_expert, xs, wgu, bgu, wd, bd)


def _combine_kernel(dest_ref, ys_ref, w_ref, x1_ref, mod_ref, npo_ref, o_ref, buf_ref, sem):
    tm = x1_ref.shape[1]

    def issue(t, carry):
        for r in range(TOP_K):
            d = dest_ref[TOP_K * t + r]
            pltpu.make_async_copy(ys_ref.at[pl.ds(d, 1)], buf_ref.at[r, pl.ds(t, 1)], sem).start(priority=r % 2)
        return carry

    lax.fori_loop(0, tm, issue, 0)
    for r in range(TOP_K):
        pltpu.make_async_copy(ys_ref.at[pl.ds(0, tm)], buf_ref.at[r], sem).wait()
    w = w_ref[0]
    ff = w[:, 0:1] * buf_ref[0]
    for r in range(1, TOP_K):
        ff = ff + w[:, r:r + 1] * buf_ref[r]
    g2 = mod_ref[0][:, 5 * D_MODEL:6 * D_MODEL]
    o_ref[0] = x1_ref[0] + g2 * _rms(ff, npo_ref[...])


def _combine(dest, ys, w4, x1, mod3, npo):
    b, s, _ = x1.shape
    tm = min(ROUTE_TILE, s)
    nt = s // tm
    tok = pl.BlockSpec((1, tm, D_MODEL), lambda bi, i: (bi, i, 0))
    return pl.pallas_call(
        _combine_kernel,
        grid=(b, nt),
        in_specs=[pl.BlockSpec((TOP_K * tm,), lambda bi, i: (bi * nt + i,), memory_space=pltpu.SMEM),
                  pl.BlockSpec(memory_space=pl.ANY),
                  pl.BlockSpec((1, tm, LANES), lambda bi, i: (bi, i, 0)),
                  tok, pl.BlockSpec((1, 1, N_MOD * D_MODEL), lambda bi, i: (bi, 0, 0)),
                  pl.BlockSpec((1, D_MODEL), lambda bi, i: (0, 0))],
        out_specs=tok,
        out_shape=jax.ShapeDtypeStruct((b, s, D_MODEL), _F32),
        scratch_shapes=[pltpu.VMEM((TOP_K, tm, D_MODEL), _F32), pltpu.SemaphoreType.DMA(())],
        compiler_params=_params(("arbitrary", "arbitrary")),
        name="moe_combine",
    )(dest, ys, w4, x1, mod3, npo)


def _moe(h2, idx4, pos4, w4, counts, wgu, bgu, wd, bd, x1, mod3, npo):
    b, s, _ = h2.shape
    t = b * s
    tr = MOE_ROW_TILE
    n_tiles = (t * TOP_K) // tr + N_EXPERTS
    cnt = counts[0, :N_EXPERTS].astype(jnp.int32)
    padded = ((cnt + tr - 1) // tr) * tr
    ends = jnp.cumsum(padded)
    starts = ends - padded
    sel = idx4[..., :TOP_K, None] == jnp.arange(N_EXPERTS, dtype=jnp.int32)
    dest = (jnp.sum(jnp.where(sel, starts, 0), axis=-1) + pos4[..., :TOP_K]).reshape(t * TOP_K)
    tile_row0 = jnp.arange(n_tiles, dtype=jnp.int32) * tr
    tile_expert = jnp.sum((ends[None, :] <= tile_row0[:, None]).astype(jnp.int32), axis=-1)
    tile_expert = jnp.minimum(tile_expert, N_EXPERTS - 1)
    xs = _scatter_rows(dest, h2.reshape(t, D_MODEL), n_tiles * tr)
    ys = _experts(tile_expert, xs, wgu, bgu, wd, bd)
    return _combine(dest, ys, w4, x1, mod3, npo)


def _rope_tables(seq, dim):
    inv = 1.0 / (ROPE_THETA ** (jnp.arange(0, dim, 2, dtype=_F32) / dim))
    ang = jnp.arange(seq, dtype=_F32)[:, None] * inv[None, :]
    ang = jnp.concatenate([ang, ang], axis=-1)
    return jnp.cos(ang), jnp.sin(ang)


def _tables(seq):
    cos_a, sin_a = _rope_tables(seq, DA_HEAD_DIM)
    half = DA_HEAD_DIM // 2
    sin_a = jnp.concatenate([-sin_a[:, :half], sin_a[:, half:]], axis=-1)
    cosa = jnp.tile(cos_a, (1, LANES // DA_HEAD_DIM))
    sina = jnp.tile(sin_a, (1, LANES // DA_HEAD_DIM))
    cos_b, sin_b = _rope_tables(seq, MLA_ROPE)
    mh = MLA_ROPE // 2
    sin_b = jnp.concatenate([-sin_b[:, :mh], sin_b[:, mh:]], axis=-1)
    pad = LANES - MLA_NOPE - MLA_ROPE
    cosm = jnp.concatenate([jnp.ones((seq, MLA_NOPE), _F32), cos_b, jnp.ones((seq, pad), _F32)], axis=-1)
    sinm = jnp.concatenate([jnp.zeros((seq, MLA_NOPE), _F32), sin_b, jnp.zeros((seq, pad), _F32)], axis=-1)
    return cosa, sina, cosm, sinm


def _pack_weights(w_in, mla_w_uq, mla_w_ukv):
    o = [0, 1024, 2048, 3072, 3456, 3712, 3744, 4768, 5792]
    qkv, cq, ckv, kr, gates = w_in[:, o[0]:o[3]], w_in[:, o[3]:o[4]], w_in[:, o[4]:o[5]], w_in[:, o[5]:o[6]], w_in[:, o[6]:o[8]]
    kr128 = jnp.pad(kr, ((0, 0), (MLA_NOPE, LANES - MLA_NOPE - MLA_ROPE)))
    wp = jnp.concatenate([qkv, gates, cq, ckv, kr128], axis=1).astype(_BF16)
    qk_dim = MLA_NOPE + MLA_ROPE
    wuq = mla_w_uq.reshape(MLA_Q_RANK, MLA_HEADS, qk_dim)
    wuq = jnp.pad(wuq, ((0, 0), (0, 0), (0, LANES - qk_dim))).reshape(MLA_Q_RANK, MLA_HEADS * LANES).astype(_BF16)
    wukv = mla_w_ukv.reshape(MLA_KV_RANK, MLA_HEADS, MLA_NOPE + MLA_V)
    wuk = jnp.pad(wukv[:, :, :MLA_NOPE], ((0, 0), (0, 0), (0, LANES - MLA_NOPE)))
    wuk = wuk.reshape(MLA_KV_RANK, MLA_HEADS * LANES).astype(_BF16)
    wuv = wukv[:, :, MLA_NOPE:].reshape(MLA_KV_RANK, MLA_HEADS * MLA_V).astype(_BF16)
    return wp, wuq, wuk, wuv


def _split_hi_lo(w):
    hi = _bf16_truncate(w)
    return jnp.stack([hi.astype(_BF16), (w - hi).astype(_BF16)])


def _layer(x, mod, lam1, lambda_init, p):
    b, s, _ = x.shape
    mod3 = mod.reshape(b, 1, N_MOD * D_MODEL)
    cosa, sina, cosm, sinm = _tables(s)
    qda, kda, vtda, qm, km, vtm, ga, gb = _inproj(
        x, mod3, p["n_pre_mix"], p["wp"], p["q_norm"], p["kv_norm"], p["wuq"], p["wuk"], p["wuv"],
        cosa, sina, cosm, sinm)
    oa = _da_attention(lam1, qda, kda, vtda, p["da_subln"], lambda_init)
    ob = _mla_attention(qm, km, vtm)
    x1, h2, idx4, pos4, w4, counts = _outp(x, mod3, oa, ob, ga, gb, p["wba"], p["wbb"], p["wo"],
                                           p["n_post_mix"], p["n_pre_ffn"], p["wr"], p["br"])
    return _moe(h2, idx4, pos4, w4, counts, p["wgu"], p["bgu"], p["wd"], p["bd"], x1, mod3, p["n_post_ffn"])


def kernel(x_prompt, x_sample, c_prompt, c_sample, w_ada, b_ada, norm_pre_mix, norm_post_mix, norm_pre_ffn, norm_post_ffn, w_in, da_lambda_q1, da_lambda_k1, da_lambda_q2, da_lambda_k2, da_subln, mla_q_norm, mla_kv_norm, mla_w_uq, mla_w_ukv, w_branch_a, w_branch_b, w_out, w_router, b_router, w_gate_up, b_gate_up, w_down, b_down):
    depth = w_ada.shape[0]
    xp, xs = x_prompt, x_sample
    bp, bs = c_prompt.shape[0], c_sample.shape[0]
    rows = -(-(bp + bs) // 8) * 8
    c_all = jnp.concatenate([c_prompt, c_sample, jnp.zeros((rows - bp - bs, D_MODEL), _F32)], axis=0)
    row = lambda v: v.reshape(1, -1)
    for l in range(depth):
        lambda_init = 0.8 - 0.6 * math.exp(-0.3 * l)
        mod, lam = _ada(c_all, w_ada[l], b_ada[l], da_lambda_q1[l], da_lambda_k1[l],
                        da_lambda_q2[l], da_lambda_k2[l], lambda_init)
        lam1 = lam[0, 0:1]
        wp, wuq, wuk, wuv = _pack_weights(w_in[l], mla_w_uq[l], mla_w_ukv[l])
        p = dict(
            n_pre_mix=row(norm_pre_mix[l]), n_post_mix=row(norm_post_mix[l]),
            n_pre_ffn=row(norm_pre_ffn[l]), n_post_ffn=row(norm_post_ffn[l]),
            wp=wp, wuq=wuq, wuk=wuk, wuv=wuv,
            q_norm=row(mla_q_norm[l]), kv_norm=row(mla_kv_norm[l]), da_subln=row(da_subln[l]),
            wba=w_branch_a[l].astype(_BF16), wbb=w_branch_b[l].astype(_BF16), wo=w_out[l].astype(_BF16),
            wr=_split_hi_lo(jnp.pad(w_router[l], ((0, 0), (0, LANES - N_EXPERTS)))),
            br=jnp.pad(row(b_router[l]), ((0, 0), (0, LANES - N_EXPERTS)), constant_values=-1e30),
            wgu=w_gate_up[l].astype(_BF16), bgu=b_gate_up[l].reshape(N_EXPERTS, 1, 2 * D_EXPERT),
            wd=w_down[l].astype(_BF16), bd=b_down[l].reshape(N_EXPERTS, 1, D_MODEL),
        )
        xp = _layer(xp, mod[0:bp], lam1, lambda_init, p)
        xs = _layer(xs, mod[bp:bp + bs], lam1, lambda_init, p)
    return (xp, xs)
```

```python
import functools
import math

import jax
import jax.numpy as jnp
from jax import lax
from jax.experimental import pallas as pl
from jax.experimental.pallas import tpu as pltpu

D_MODEL = 1024
DA_HEADS = 8
DA_HEAD_DIM = 64
DA_V_DIM = 128
MLA_HEADS = 16
MLA_NOPE = 64
MLA_ROPE = 32
MLA_V = 64
MLA_Q_RANK = 384
MLA_KV_RANK = 256
N_EXPERTS = 32
TOP_K = 4
D_EXPERT = 1024
SWIGLU_LIMIT = 7.0
SWIGLU_ALPHA = 1.702
ROPE_THETA = 10000.0
EPS = 1e-6
N_MOD = 6

LANES = 128
F32_SUBLANES = 8
BF16_SUBLANES = 16
VMEM_LIMIT = 56 * 1024 * 1024
LOG2E = 1.4426950408889634

_QKV0, _GATE0, _CQ0, _CKV0, _KR0, _WP = 0, 3072, 5120, 5504, 5760, 5888

_F32 = jnp.float32
_BF16 = jnp.bfloat16
_NT = (((1,), (1,)), ((), ()))


def _dot(a, b):
    return jnp.dot(a, b, preferred_element_type=_F32)


def _dot_nt(a, b):
    return lax.dot_general(a, b, _NT, preferred_element_type=_F32)


def _bf16_truncate(x):
    bits = lax.bitcast_convert_type(x, jnp.uint32) & jnp.uint32(0xFFFF0000)
    return lax.bitcast_convert_type(bits, _F32)


def _rms(x, w):
    return x * lax.rsqrt(jnp.mean(x * x, axis=-1, keepdims=True) + EPS) * w


def _const_spec(shape):
    nd = len(shape)
    return pl.BlockSpec(shape, lambda *_: (0,) * nd, pipeline_mode=pl.Buffered(1))


def _params(sem):
    return pltpu.CompilerParams(dimension_semantics=sem, vmem_limit_bytes=VMEM_LIMIT)


def _ada_kernel(c_ref, w_ref, b_ref, lq1_ref, lk1_ref, lq2_ref, lk2_ref, mod_ref, lam_ref, *, lambda_init):
    c = c_ref[...]
    s = c * jax.nn.sigmoid(c)
    mod_ref[...] = jnp.dot(s, w_ref[...], preferred_element_type=_F32,
                           precision=lax.Precision.HIGHEST) + b_ref[...]
    d1 = jnp.sum(lq1_ref[...] * lk1_ref[...], axis=-1, keepdims=True)
    d2 = jnp.sum(lq2_ref[...] * lk2_ref[...], axis=-1, keepdims=True)
    lam = jnp.exp(d1) - jnp.exp(d2) + lambda_init
    lam_ref[...] = jnp.broadcast_to(lam, lam_ref.shape)


def _ada(c_all, w_ada, b_ada, lq1, lk1, lq2, lk2, lambda_init):
    rows = c_all.shape[0]
    ncol = w_ada.shape[1]
    tn = D_MODEL
    vec = lambda r: r.reshape(1, -1)
    small = pl.BlockSpec((1, DA_HEAD_DIM), lambda j: (0, 0))
    return pl.pallas_call(
        functools.partial(_ada_kernel, lambda_init=lambda_init),
        grid=(ncol // tn,),
        in_specs=[pl.BlockSpec((rows, D_MODEL), lambda j: (0, 0)),
                  pl.BlockSpec((D_MODEL, tn), lambda j: (0, j)),
                  pl.BlockSpec((1, tn), lambda j: (0, j)),
                  small, small, small, small],
        out_specs=[pl.BlockSpec((rows, tn), lambda j: (0, j)),
                   pl.BlockSpec((8, LANES), lambda j: (0, 0))],
        out_shape=[jax.ShapeDtypeStruct((rows, ncol), _F32),
                   jax.ShapeDtypeStruct((8, LANES), _F32)],
        compiler_params=_params(("arbitrary",)),
        name="ada",
    )(c_all, w_ada, vec(b_ada), vec(lq1), vec(lk1), vec(lq2), vec(lk2))


def _rope128(x, cos, sin_signed, first_half, shift_lo, shift_hi):
    r = jnp.where(first_half, pltpu.roll(x, shift_lo, 1), pltpu.roll(x, shift_hi, 1))
    return x * cos + r * sin_signed


def _inproj_kernel(x_ref, mod_ref, n1_ref, wp_ref, qn_ref, kvn_ref, wuq_ref, wuk_ref, wuv_ref,
                   cosa_ref, sina_ref, cosm_ref, sinm_ref,
                   qda_ref, kda_ref, vtda_ref, qm_ref, km_ref, vtm_ref, ga_ref, gb_ref,
                   *, da_scale, mla_scale):
    x = x_ref[0]
    mod = mod_ref[0]
    sh1 = mod[:, 0:D_MODEL]
    sc1 = mod[:, D_MODEL:2 * D_MODEL]
    h = _rms(x, n1_ref[...]) * (1.0 + sc1) + sh1
    hb = h.astype(_BF16)

    lane = lax.broadcasted_iota(jnp.int32, (x.shape[0], LANES), 1)
    da_first = (lane % DA_HEAD_DIM) < (DA_HEAD_DIM // 2)
    pe_mid = MLA_NOPE + MLA_ROPE // 2
    mla_first = lane < pe_mid
    cosa, sina = cosa_ref[...], sina_ref[...]
    cosm, sinm = cosm_ref[...], sinm_ref[...]
    half = DA_HEAD_DIM // 2
    mhalf = MLA_ROPE // 2

    for pair in range(DA_HEADS // 2):
        c0 = 2 * pair * LANES
        zq = _dot(hb, wp_ref[:, _QKV0 + c0:_QKV0 + c0 + 2 * LANES])
        zk = _dot(hb, wp_ref[:, _QKV0 + D_MODEL + c0:_QKV0 + D_MODEL + c0 + 2 * LANES])
        zv = _dot(hb, wp_ref[:, _QKV0 + 2 * D_MODEL + c0:_QKV0 + 2 * D_MODEL + c0 + 2 * LANES])
        for t in range(2):
            hd = 2 * pair + t
            sl = slice(t * LANES, (t + 1) * LANES)
            q = _rope128(zq[:, sl], cosa, sina, da_first, LANES - half, half)
            qda_ref[0, hd] = (q * da_scale).astype(_BF16)
            k = _rope128(zk[:, sl], cosa, sina, da_first, LANES - half, half)
            kda_ref[0, hd] = k.astype(_BF16)
            vtda_ref[0, hd] = zv[:, sl].T.astype(_BF16)

    for blk in range(2 * D_MODEL // 512):
        z = _dot(hb, wp_ref[:, _GATE0 + blk * 512:_GATE0 + (blk + 1) * 512])
        g = jax.nn.sigmoid(z).astype(_BF16)
        if blk < D_MODEL // 512:
            ga_ref[0, :, blk * 512:(blk + 1) * 512] = g
        else:
            o = blk * 512 - D_MODEL
            gb_ref[0, :, o:o + 512] = g

    cq = _dot(hb, wp_ref[:, _CQ0:_CQ0 + MLA_Q_RANK])
    cqn = _rms(cq, qn_ref[...]).astype(_BF16)
    for pair in range(MLA_HEADS // 2):
        z = _dot(cqn, wuq_ref[:, 2 * pair * LANES:(2 * pair + 2) * LANES])
        for t in range(2):
            q = _rope128(z[:, t * LANES:(t + 1) * LANES], cosm, sinm, mla_first, LANES - mhalf, mhalf)
            qm_ref[0, 2 * pair + t] = (q * mla_scale).astype(_BF16)

    ckv = _dot(hb, wp_ref[:, _CKV0:_CKV0 + MLA_KV_RANK])
    ckvn = _rms(ckv, kvn_ref[...]).astype(_BF16)
    kr = _dot(hb, wp_ref[:, _KR0:_KR0 + LANES])
    kr = _rope128(kr, cosm, sinm, mla_first, LANES - mhalf, mhalf)
    for pair in range(MLA_HEADS // 2):
        zk = _dot(ckvn, wuk_ref[:, 2 * pair * LANES:(2 * pair + 2) * LANES])
        for t in range(2):
            km_ref[0, 2 * pair + t] = (zk[:, t * LANES:(t + 1) * LANES] + kr).astype(_BF16)
        vt = _dot(ckvn, wuv_ref[:, pair * LANES:(pair + 1) * LANES]).T.astype(_BF16)
        vtm_ref[0, 2 * pair] = vt[0:MLA_V]
        vtm_ref[0, 2 * pair + 1] = vt[MLA_V:2 * MLA_V]


def _inproj(x, mod3, n1, wp, qn, kvn, wuq, wuk, wuv, cosa, sina, cosm, sinm):
    b, s, _ = x.shape
    tm = min(512, s)
    bf = lambda *shape: jax.ShapeDtypeStruct(shape, _BF16)
    tok = pl.BlockSpec((1, tm, D_MODEL), lambda bi, i: (bi, i, 0))
    tab = pl.BlockSpec((tm, LANES), lambda bi, i: (i, 0))
    hrow = lambda nh: pl.BlockSpec((1, nh, tm, LANES), lambda bi, i: (bi, 0, i, 0))
    hcol = lambda nh, r: pl.BlockSpec((1, nh, r, tm), lambda bi, i: (bi, 0, 0, i))
    kern = functools.partial(_inproj_kernel,
                             da_scale=DA_HEAD_DIM ** -0.5 * LOG2E,
                             mla_scale=(MLA_NOPE + MLA_ROPE) ** -0.5 * LOG2E)
    return pl.pallas_call(
        kern,
        grid=(b, s // tm),
        in_specs=[tok,
                  pl.BlockSpec((1, 1, N_MOD * D_MODEL), lambda bi, i: (bi, 0, 0)),
                  _const_spec((1, D_MODEL)),
                  _const_spec((D_MODEL, _WP)),
                  _const_spec((1, MLA_Q_RANK)),
                  _const_spec((1, MLA_KV_RANK)),
                  _const_spec((MLA_Q_RANK, MLA_HEADS * LANES)),
                  _const_spec((MLA_KV_RANK, MLA_HEADS * LANES)),
                  _const_spec((MLA_KV_RANK, MLA_HEADS * MLA_V)),
                  tab, tab, tab, tab],
        out_specs=[hrow(DA_HEADS), hrow(DA_HEADS), hcol(DA_HEADS, DA_V_DIM),
                   hrow(MLA_HEADS), hrow(MLA_HEADS), hcol(MLA_HEADS, MLA_V), tok, tok],
        out_shape=[bf(b, DA_HEADS, s, LANES), bf(b, DA_HEADS, s, LANES), bf(b, DA_HEADS, DA_V_DIM, s),
                   bf(b, MLA_HEADS, s, LANES), bf(b, MLA_HEADS, s, LANES), bf(b, MLA_HEADS, MLA_V, s),
                   bf(b, s, D_MODEL), bf(b, s, D_MODEL)],
        compiler_params=_params(("parallel", "parallel")),
        name="inproj",
    )(x, mod3, n1, wp, qn, kvn, wuq, wuk, wuv, cosa, sina, cosm, sinm)


DA_KEY_SUB, DA_KEY_BLOCK = 512, 2048
MLA_KEY_SUB, MLA_KEY_BLOCK = 1024, 2048
QUERY_TILE = 512


def _flash_items(n_heads, tk, key_sub, scores, vt_of, m_ref, acc_ref):
    sub = min(key_sub, tk)
    items = [(hd, pl.ds(c * sub, sub)) for c in range(tk // sub) for hd in range(n_heads)]

    def stage_a(hd, ks):
        st = scores(hd, ks)
        m_old = m_ref[hd]
        m_new = jnp.maximum(m_old, jnp.max(st, axis=0, keepdims=True))
        m_ref[hd] = m_new
        return st, m_new, jnp.exp2(m_old - m_new)

    def stage_c(hd, ks, p, alpha):
        vt = vt_of(hd, ks)
        ones = jnp.ones((BF16_SUBLANES, vt.shape[1]), _BF16)
        pv = _dot(jnp.concatenate([vt, ones], axis=0), p)
        acc_ref[hd] = alpha * acc_ref[hd] + pv

    a_out, b_out = {}, {}
    for t in range(len(items) + 2):
        if t < len(items):
            a_out[t] = stage_a(*items[t])
        if 0 <= t - 1 < len(items):
            st, m_new, alpha = a_out.pop(t - 1)
            b_out[t - 1] = (jnp.exp2(st - m_new).astype(_BF16), alpha)
        if 0 <= t - 2 < len(items):
            stage_c(*items[t - 2], *b_out.pop(t - 2))


def _da_kernel(lam_ref, q_ref, k_ref, vt_ref, sub_ref, o_ref, w_ref, m_ref, acc_ref, *, out_scale):
    j = pl.program_id(2)
    tq = q_ref.shape[2]

    @pl.when(j == 0)
    def _():
        lane = lax.broadcasted_iota(jnp.int32, (tq, LANES), 1)
        first = lane < DA_HEAD_DIM
        for hd in range(DA_HEADS):
            q = q_ref[0, hd]
            w_ref[hd, 0:tq] = jnp.where(first, q, jnp.zeros_like(q))
            w_ref[hd, tq:2 * tq] = jnp.where(first, jnp.zeros_like(q), q)
        m_ref[...] = jnp.full(m_ref.shape, -jnp.inf, _F32)
        acc_ref[...] = jnp.zeros(acc_ref.shape, _F32)

    _flash_items(DA_HEADS, k_ref.shape[2], DA_KEY_SUB,
                 lambda hd, ks: _dot_nt(k_ref[0, hd, ks, :], w_ref[hd]),
                 lambda hd, ks: vt_ref[0, hd, :, ks], m_ref, acc_ref)

    @pl.when(j == pl.num_programs(2) - 1)
    def _():
        lam = lam_ref[0]
        for hd in range(DA_HEADS):
            a = acc_ref[hd]
            o = a[0:DA_V_DIM] / a[DA_V_DIM:DA_V_DIM + 1]
            od = (o[:, 0:tq] - lam * o[:, tq:2 * tq]).T
            o_ref[0, :, hd * LANES:(hd + 1) * LANES] = (_rms(od, sub_ref[...]) * out_scale).astype(o_ref.dtype)


def _da_attention(lam1, q, k, vt, subln, lambda_init):
    b, nh, s, _ = q.shape
    tq, tk = min(QUERY_TILE, s), min(DA_KEY_BLOCK, s)
    return pl.pallas_call(
        functools.partial(_da_kernel, out_scale=1.0 - lambda_init),
        grid=(b, s // tq, s // tk),
        in_specs=[pl.BlockSpec(memory_space=pltpu.SMEM),
                  pl.BlockSpec((1, nh, tq, LANES), lambda bi, i, j: (bi, 0, i, 0)),
                  pl.BlockSpec((1, nh, tk, LANES), lambda bi, i, j: (bi, 0, j, 0)),
                  pl.BlockSpec((1, nh, DA_V_DIM, tk), lambda bi, i, j: (bi, 0, 0, j)),
                  pl.BlockSpec((1, DA_V_DIM), lambda bi, i, j: (0, 0))],
        out_specs=pl.BlockSpec((1, tq, D_MODEL), lambda bi, i, j: (bi, i, 0)),
        out_shape=jax.ShapeDtypeStruct((b, s, D_MODEL), _BF16),
        scratch_shapes=[pltpu.VMEM((nh, 2 * tq, LANES), _BF16),
                        pltpu.VMEM((nh, 1, 2 * tq), _F32),
                        pltpu.VMEM((nh, DA_V_DIM + BF16_SUBLANES, 2 * tq), _F32)],
        compiler_params=_params(("parallel", "parallel", "arbitrary")),
        name="da_attention",
    )(lam1, q, k, vt, subln)


def _mla_kernel(q_ref, k_ref, vt_ref, o_ref, m_ref, acc_ref):
    j = pl.program_id(2)

    @pl.when(j == 0)
    def _():
        m_ref[...] = jnp.full(m_ref.shape, -jnp.inf, _F32)
        acc_ref[...] = jnp.zeros(acc_ref.shape, _F32)

    _flash_items(MLA_HEADS, k_ref.shape[2], MLA_KEY_SUB,
                 lambda hd, ks: _dot_nt(k_ref[0, hd, ks, :], q_ref[0, hd]),
                 lambda hd, ks: vt_ref[0, hd, :, ks], m_ref, acc_ref)

    @pl.when(j == pl.num_programs(2) - 1)
    def _():
        for pair in range(MLA_HEADS // 2):
            outs = []
            for hd in (2 * pair, 2 * pair + 1):
                a = acc_ref[hd]
                outs.append(a[0:MLA_V] / a[MLA_V:MLA_V + 1])
            o_ref[0, :, pair * LANES:(pair + 1) * LANES] = jnp.concatenate(outs, axis=0).T.astype(o_ref.dtype)


def _mla_attention(q, k, vt):
    b, nh, s, _ = q.shape
    tq, tk = min(QUERY_TILE, s), min(MLA_KEY_BLOCK, s)
    return pl.pallas_call(
        _mla_kernel,
        grid=(b, s // tq, s // tk),
        in_specs=[pl.BlockSpec((1, nh, tq, LANES), lambda bi, i, j: (bi, 0, i, 0)),
                  pl.BlockSpec((1, nh, tk, LANES), lambda bi, i, j: (bi, 0, j, 0)),
                  pl.BlockSpec((1, nh, MLA_V, tk), lambda bi, i, j: (bi, 0, 0, j))],
        out_specs=pl.BlockSpec((1, tq, D_MODEL), lambda bi, i, j: (bi, i, 0)),
        out_shape=jax.ShapeDtypeStruct((b, s, D_MODEL), _BF16),
        scratch_shapes=[pltpu.VMEM((nh, 1, tq), _F32),
                        pltpu.VMEM((nh, MLA_V + BF16_SUBLANES, tq), _F32)],
        compiler_params=_params(("parallel", "parallel", "arbitrary")),
        name="mla_attention",
    )(q, k, vt)


def _outp_kernel(x_ref, mod_ref, oa_ref, ob_ref, ga_ref, gb_ref, wba_ref, wbb_ref, wo_ref,
                 npm_ref, npf_ref, wr_ref, br_ref,
                 x1_ref, h2_ref, idx_ref, pos_ref, w_ref, cnt_ref, carry_ref):
    first = jnp.logical_and(pl.program_id(0) == 0, pl.program_id(1) == 0)

    @pl.when(first)
    def _():
        carry_ref[...] = jnp.zeros(carry_ref.shape, _F32)

    mod = mod_ref[0]
    g1 = mod[:, 2 * D_MODEL:3 * D_MODEL]
    sh2 = mod[:, 3 * D_MODEL:4 * D_MODEL]
    sc2 = mod[:, 4 * D_MODEL:5 * D_MODEL]
    merged = (ga_ref[0].astype(_F32) * _dot(oa_ref[0], wba_ref[...])
              + gb_ref[0].astype(_F32) * _dot(ob_ref[0], wbb_ref[...]))
    mix = _dot(merged.astype(_BF16), wo_ref[...])
    x1 = x_ref[0] + g1 * _rms(mix, npm_ref[...])
    x1_ref[0] = x1
    h2 = _rms(x1, npf_ref[...]) * (1.0 + sc2) + sh2
    h2_ref[0] = h2

    h_hi = _bf16_truncate(h2)
    h_lo = (h2 - h_hi).astype(_BF16)
    h_hi = h_hi.astype(_BF16)
    logits = _dot(h_hi, wr_ref[0]) + (_dot(h_lo, wr_ref[0]) + _dot(h_hi, wr_ref[1])) + br_ref[...]
    tm = logits.shape[0]
    lane = lax.broadcasted_iota(jnp.int32, logits.shape, 1)
    vals = logits
    sels, idxs, exps = [], [], []
    top = None
    for _ in range(TOP_K):
        mx = jnp.max(vals, axis=-1, keepdims=True)
        idx = jnp.min(jnp.where(vals == mx, lane, LANES), axis=-1, keepdims=True)
        sel = lane == idx
        top = mx if top is None else top
        sels.append(sel)
        idxs.append(idx)
        exps.append(jnp.exp(mx - top))
        vals = jnp.where(sel, -jnp.inf, vals)
    den = exps[0] + exps[1] + exps[2] + exps[3]

    onehot = jnp.zeros(logits.shape, _F32)
    for sel in sels:
        onehot = onehot + jnp.where(sel, 1.0, 0.0)
    r_i = lax.broadcasted_iota(jnp.int32, (tm, tm), 0)
    c_i = lax.broadcasted_iota(jnp.int32, (tm, tm), 1)
    earlier = jnp.where(r_i > c_i, 1.0, 0.0).astype(_BF16)
    rank = _dot(earlier, onehot.astype(_BF16)) + carry_ref[0:1, :]
    idx4 = jnp.zeros(logits.shape, jnp.int32)
    pos4 = jnp.zeros(logits.shape, jnp.int32)
    w4 = jnp.zeros(logits.shape, _F32)
    for r in range(TOP_K):
        pos = jnp.sum(jnp.where(sels[r], rank, 0.0), axis=-1, keepdims=True)
        idx4 = jnp.where(lane == r, idxs[r], idx4)
        pos4 = jnp.where(lane == r, pos.astype(jnp.int32), pos4)
        w4 = jnp.where(lane == r, exps[r] / den, w4)
    idx_ref[0] = idx4
    pos_ref[0] = pos4
    w_ref[0] = w4
    carry_ref[...] = carry_ref[...] + jnp.sum(onehot, axis=0, keepdims=True)
    cnt_ref[...] = carry_ref[...]


def _outp(x, mod3, oa, ob, ga, gb, wba, wbb, wo, npm, npf, wr, br):
    b, s, _ = x.shape
    tm = min(512, s)
    tok = pl.BlockSpec((1, tm, D_MODEL), lambda bi, i: (bi, i, 0))
    meta = pl.BlockSpec((1, tm, LANES), lambda bi, i: (bi, i, 0))
    sq = _const_spec((D_MODEL, D_MODEL))
    row = _const_spec((1, D_MODEL))
    return pl.pallas_call(
        _outp_kernel,
        grid=(b, s // tm),
        in_specs=[tok, pl.BlockSpec((1, 1, N_MOD * D_MODEL), lambda bi, i: (bi, 0, 0)),
                  tok, tok, tok, tok, sq, sq, sq, row, row,
                  _const_spec((2, D_MODEL, LANES)), _const_spec((1, LANES))],
        out_specs=[tok, tok, meta, meta, meta, pl.BlockSpec((8, LANES), lambda bi, i: (0, 0))],
        out_shape=[jax.ShapeDtypeStruct((b, s, D_MODEL), _F32),
                   jax.ShapeDtypeStruct((b, s, D_MODEL), _F32),
                   jax.ShapeDtypeStruct((b, s, LANES), jnp.int32),
                   jax.ShapeDtypeStruct((b, s, LANES), jnp.int32),
                   jax.ShapeDtypeStruct((b, s, LANES), _F32),
                   jax.ShapeDtypeStruct((8, LANES), _F32)],
        scratch_shapes=[pltpu.VMEM((8, LANES), _F32)],
        compiler_params=_params(("arbitrary", "arbitrary")),
        name="outp",
    )(x, mod3, oa, ob, ga, gb, wba, wbb, wo, npm, npf, wr, br)


MOE_ROW_TILE = 512
ROUTE_TILE = 512


def _scatter_kernel(fill_ref, dest_ref, h_ref, xs_ref, zero_ref, sem, zero_sem):
    tm = h_ref.shape[0]
    tr = zero_ref.shape[0]

    @pl.when(pl.program_id(0) == 0)
    def _():
        zero_ref[...] = jnp.zeros(zero_ref.shape, _F32)

        def fill_from(row):
            fill = pltpu.make_async_copy(zero_ref, xs_ref.at[pl.ds(row, tr)], zero_sem)
            fill.start()
            fill.wait()

        for e in range(N_EXPERTS):
            fill_from(pl.multiple_of(fill_ref[e], F32_SUBLANES))

        def fill_tile(g, carry):
            fill_from(pl.multiple_of(g * tr, tr))
            return carry

        lax.fori_loop(fill_ref[N_EXPERTS], xs_ref.shape[0] // tr, fill_tile, 0)

    def issue(t, carry):
        for r in range(TOP_K):
            d = dest_ref[TOP_K * t + r]
            pltpu.make_async_copy(h_ref.at[pl.ds(t, 1)], xs_ref.at[pl.ds(d, 1)], sem).start()
        return carry

    lax.fori_loop(0, tm, issue, 0)
    for _ in range(TOP_K):
        pltpu.make_async_copy(h_ref, xs_ref.at[pl.ds(0, tm)], sem).wait()


def _scatter_rows(fill, dest, h2, n_rows):
    t = h2.shape[0]
    tm = min(ROUTE_TILE, t)
    return pl.pallas_call(
        _scatter_kernel,
        grid_spec=pltpu.PrefetchScalarGridSpec(
            num_scalar_prefetch=1,
            grid=(t // tm,),
            in_specs=[pl.BlockSpec((TOP_K * tm,), lambda i, fill: (i,), memory_space=pltpu.SMEM),
                      pl.BlockSpec((tm, D_MODEL), lambda i, fill: (i, 0))],
            out_specs=pl.BlockSpec(memory_space=pl.ANY),
            scratch_shapes=[pltpu.VMEM((MOE_ROW_TILE, D_MODEL), _F32),
                            pltpu.SemaphoreType.DMA(()), pltpu.SemaphoreType.DMA(())]),
        out_shape=jax.ShapeDtypeStruct((n_rows, D_MODEL), _F32),
        compiler_params=_params(("arbitrary",)),
        name="moe_scatter",
    )(fill, dest, h2)


def _experts_kernel(te_ref, xs_ref, wgu_ref, bgu_ref, wd_ref, bd_ref, ys_ref):
    del te_ref
    gu = _dot(xs_ref[...].astype(_BF16), wgu_ref[0].astype(_BF16)) + bgu_ref[0]
    glu = jnp.minimum(gu[:, :D_EXPERT], SWIGLU_LIMIT)
    lin = jnp.clip(gu[:, D_EXPERT:], -SWIGLU_LIMIT, SWIGLU_LIMIT)
    act = glu * jax.nn.sigmoid(SWIGLU_ALPHA * glu) * (lin + 1.0)
    ys_ref[...] = _dot(act.astype(_BF16), wd_ref[0].astype(_BF16)) + bd_ref[0]


def _experts(tile_expert, xs, wgu, bgu, wd, bd):
    n_rows = xs.shape[0]
    tr = MOE_ROW_TILE
    rows = pl.BlockSpec((tr, D_MODEL), lambda g, te: (g, 0))
    return pl.pallas_call(
        _experts_kernel,
        grid_spec=pltpu.PrefetchScalarGridSpec(
            num_scalar_prefetch=1,
            grid=(n_rows // tr,),
            in_specs=[rows,
                      pl.BlockSpec((1, D_MODEL, 2 * D_EXPERT), lambda g, te: (te[g], 0, 0)),
                      pl.BlockSpec((1, 1, 2 * D_EXPERT), lambda g, te: (te[g], 0, 0)),
                      pl.BlockSpec((1, D_EXPERT, D_MODEL), lambda g, te: (te[g], 0, 0)),
                      pl.BlockSpec((1, 1, D_MODEL), lambda g, te: (te[g], 0, 0))],
            out_specs=rows),
        out_shape=jax.ShapeDtypeStruct((n_rows, D_MODEL), _F32),
        compiler_params=_params(("arbitrary",)),
        name="moe_experts",
    )(tile_expert, xs, wgu, bgu, wd, bd)


def _combine_kernel(dest_ref, ys_ref, w_ref, x1_ref, mod_ref, npo_ref, o_ref, buf_ref, sem):
    tm = x1_ref.shape[1]

    def issue(t, carry):
        for r in range(TOP_K):
            d = dest_ref[TOP_K * t + r]
            pltpu.make_async_copy(ys_ref.at[pl.ds(d, 1)], buf_ref.at[r, pl.ds(t, 1)], sem).start()
        return carry

    lax.fori_loop(0, tm, issue, 0)
    for r in range(TOP_K):
        pltpu.make_async_copy(ys_ref.at[pl.ds(0, tm)], buf_ref.at[r], sem).wait()
    w = w_ref[0]
    ff = w[:, 0:1] * buf_ref[0]
    for r in range(1, TOP_K):
        ff = ff + w[:, r:r + 1] * buf_ref[r]
    g2 = mod_ref[0][:, 5 * D_MODEL:6 * D_MODEL]
    o_ref[0] = x1_ref[0] + g2 * _rms(ff, npo_ref[...])


def _combine(dest, ys, w4, x1, mod3, npo):
    b, s, _ = x1.shape
    tm = min(ROUTE_TILE, s)
    nt = s // tm
    tok = pl.BlockSpec((1, tm, D_MODEL), lambda bi, i: (bi, i, 0))
    return pl.pallas_call(
        _combine_kernel,
        grid=(b, nt),
        in_specs=[pl.BlockSpec((TOP_K * tm,), lambda bi, i: (bi * nt + i,), memory_space=pltpu.SMEM),
                  pl.BlockSpec(memory_space=pl.ANY),
                  pl.BlockSpec((1, tm, LANES), lambda bi, i: (bi, i, 0)),
                  tok, pl.BlockSpec((1, 1, N_MOD * D_MODEL), lambda bi, i: (bi, 0, 0)),
                  pl.BlockSpec((1, D_MODEL), lambda bi, i: (0, 0))],
        out_specs=tok,
        out_shape=jax.ShapeDtypeStruct((b, s, D_MODEL), _F32),
        scratch_shapes=[pltpu.VMEM((TOP_K, tm, D_MODEL), _F32), pltpu.SemaphoreType.DMA(())],
        compiler_params=_params(("arbitrary", "arbitrary")),
        name="moe_combine",
    )(dest, ys, w4, x1, mod3, npo)


def _moe(h2, idx4, pos4, w4, counts, wgu, bgu, wd, bd, x1, mod3, npo):
    b, s, _ = h2.shape
    t = b * s
    tr = MOE_ROW_TILE
    n_tiles = (t * TOP_K) // tr + N_EXPERTS
    cnt = counts[0, :N_EXPERTS].astype(jnp.int32)
    padded = ((cnt + tr - 1) // tr) * tr
    ends = jnp.cumsum(padded)
    starts = ends - padded
    sel = idx4[..., :TOP_K, None] == jnp.arange(N_EXPERTS, dtype=jnp.int32)
    dest = (jnp.sum(jnp.where(sel, starts, 0), axis=-1) + pos4[..., :TOP_K]).reshape(t * TOP_K)
    tile_row0 = jnp.arange(n_tiles, dtype=jnp.int32) * tr
    tile_expert = jnp.sum((ends[None, :] <= tile_row0[:, None]).astype(jnp.int32), axis=-1)
    tile_expert = jnp.minimum(tile_expert, N_EXPERTS - 1)
    fill = jnp.concatenate([(starts + cnt) // F32_SUBLANES * F32_SUBLANES, ends[-1:] // tr])
    xs = _scatter_rows(fill, dest, h2.reshape(t, D_MODEL), n_tiles * tr)
    ys = _experts(tile_expert, xs, wgu, bgu, wd, bd)
    return _combine(dest, ys, w4, x1, mod3, npo)


def _rope_tables(seq, dim):
    inv = 1.0 / (ROPE_THETA ** (jnp.arange(0, dim, 2, dtype=_F32) / dim))
    ang = jnp.arange(seq, dtype=_F32)[:, None] * inv[None, :]
    ang = jnp.concatenate([ang, ang], axis=-1)
    return jnp.cos(ang), jnp.sin(ang)


def _tables(seq):
    cos_a, sin_a = _rope_tables(seq, DA_HEAD_DIM)
    half = DA_HEAD_DIM // 2
    sin_a = jnp.concatenate([-sin_a[:, :half], sin_a[:, half:]], axis=-1)
    cosa = jnp.tile(cos_a, (1, LANES // DA_HEAD_DIM))
    sina = jnp.tile(sin_a, (1, LANES // DA_HEAD_DIM))
    cos_b, sin_b = _rope_tables(seq, MLA_ROPE)
    mh = MLA_ROPE // 2
    sin_b = jnp.concatenate([-sin_b[:, :mh], sin_b[:, mh:]], axis=-1)
    pad = LANES - MLA_NOPE - MLA_ROPE
    cosm = jnp.concatenate([jnp.ones((seq, MLA_NOPE), _F32), cos_b, jnp.ones((seq, pad), _F32)], axis=-1)
    sinm = jnp.concatenate([jnp.zeros((seq, MLA_NOPE), _F32), sin_b, jnp.zeros((seq, pad), _F32)], axis=-1)
    return cosa, sina, cosm, sinm


def _pack_weights(w_in, mla_w_uq, mla_w_ukv):
    o = [0, 1024, 2048, 3072, 3456, 3712, 3744, 4768, 5792]
    qkv, cq, ckv, kr, gates = w_in[:, o[0]:o[3]], w_in[:, o[3]:o[4]], w_in[:, o[4]:o[5]], w_in[:, o[5]:o[6]], w_in[:, o[6]:o[8]]
    kr128 = jnp.pad(kr, ((0, 0), (MLA_NOPE, LANES - MLA_NOPE - MLA_ROPE)))
    wp = jnp.concatenate([qkv, gates, cq, ckv, kr128], axis=1).astype(_BF16)
    qk_dim = MLA_NOPE + MLA_ROPE
    wuq = mla_w_uq.reshape(MLA_Q_RANK, MLA_HEADS, qk_dim)
    wuq = jnp.pad(wuq, ((0, 0), (0, 0), (0, LANES - qk_dim))).reshape(MLA_Q_RANK, MLA_HEADS * LANES).astype(_BF16)
    wukv = mla_w_ukv.reshape(MLA_KV_RANK, MLA_HEADS, MLA_NOPE + MLA_V)
    wuk = jnp.pad(wukv[:, :, :MLA_NOPE], ((0, 0), (0, 0), (0, LANES - MLA_NOPE)))
    wuk = wuk.reshape(MLA_KV_RANK, MLA_HEADS * LANES).astype(_BF16)
    wuv = wukv[:, :, MLA_NOPE:].reshape(MLA_KV_RANK, MLA_HEADS * MLA_V).astype(_BF16)
    return wp, wuq, wuk, wuv


def _split_hi_lo(w):
    hi = _bf16_truncate(w)
    return jnp.stack([hi.astype(_BF16), (w - hi).astype(_BF16)])


def _layer(x, mod, lam1, lambda_init, p):
    b, s, _ = x.shape
    mod3 = mod.reshape(b, 1, N_MOD * D_MODEL)
    cosa, sina, cosm, sinm = _tables(s)
    qda, kda, vtda, qm, km, vtm, ga, gb = _inproj(
        x, mod3, p["n_pre_mix"], p["wp"], p["q_norm"], p["kv_norm"], p["wuq"], p["wuk"], p["wuv"],
        cosa, sina, cosm, sinm)
    oa = _da_attention(lam1, qda, kda, vtda, p["da_subln"], lambda_init)
    ob = _mla_attention(qm, km, vtm)
    x1, h2, idx4, pos4, w4, counts = _outp(x, mod3, oa, ob, ga, gb, p["wba"], p["wbb"], p["wo"],
                                           p["n_post_mix"], p["n_pre_ffn"], p["wr"], p["br"])
    return _moe(h2, idx4, pos4, w4, counts, p["wgu"], p["bgu"], p["wd"], p["bd"], x1, mod3, p["n_post_ffn"])


def kernel(x_prompt, x_sample, c_prompt, c_sample, w_ada, b_ada, norm_pre_mix, norm_post_mix, norm_pre_ffn, norm_post_ffn, w_in, da_lambda_q1, da_lambda_k1, da_lambda_q2, da_lambda_k2, da_subln, mla_q_norm, mla_kv_norm, mla_w_uq, mla_w_ukv, w_branch_a, w_branch_b, w_out, w_router, b_router, w_gate_up, b_gate_up, w_down, b_down):
    depth = w_ada.shape[0]
    xp, xs = x_prompt, x_sample
    bp, bs = c_prompt.shape[0], c_sample.shape[0]
    rows = -(-(bp + bs) // 8) * 8
    c_all = jnp.concatenate([c_prompt, c_sample, jnp.zeros((rows - bp - bs, D_MODEL), _F32)], axis=0)
    row = lambda v: v.reshape(1, -1)
    for l in range(depth):
        lambda_init = 0.8 - 0.6 * math.exp(-0.3 * l)
        mod, lam = _ada(c_all, w_ada[l], b_ada[l], da_lambda_q1[l], da_lambda_k1[l],
                        da_lambda_q2[l], da_lambda_k2[l], lambda_init)
        lam1 = lam[0, 0:1]
        wp, wuq, wuk, wuv = _pack_weights(w_in[l], mla_w_uq[l], mla_w_ukv[l])
        p = dict(
            n_pre_mix=row(norm_pre_mix[l]), n_post_mix=row(norm_post_mix[l]),
            n_pre_ffn=row(norm_pre_ffn[l]), n_post_ffn=row(norm_post_ffn[l]),
            wp=wp, wuq=wuq, wuk=wuk, wuv=wuv,
            q_norm=row(mla_q_norm[l]), kv_norm=row(mla_kv_norm[l]), da_subln=row(da_subln[l]),
            wba=w_branch_a[l].astype(_BF16), wbb=w_branch_b[l].astype(_BF16), wo=w_out[l].astype(_BF16),
            wr=_split_hi_lo(jnp.pad(w_router[l], ((0, 0), (0, LANES - N_EXPERTS)))),
            br=jnp.pad(row(b_router[l]), ((0, 0), (0, LANES - N_EXPERTS)), constant_values=-1e30),
            wgu=w_gate_up[l], bgu=b_gate_up[l].reshape(N_EXPERTS, 1, 2 * D_EXPERT),
            wd=w_down[l], bd=b_down[l].reshape(N_EXPERTS, 1, D_MODEL),
        )
        xp = _layer(xp, mod[0:bp], lam1, lambda_init, p)
        xs = _layer(xs, mod[bp:bp + bs], lam1, lambda_init, p)
    return (xp, xs)
```

```python
import functools
import math

import jax
import jax.numpy as jnp
from jax import lax
from jax.experimental import pallas as pl
from jax.experimental.pallas import tpu as pltpu

D_MODEL = 1024
DA_HEADS = 8
DA_HEAD_DIM = 64
DA_V_DIM = 128
MLA_HEADS = 16
MLA_NOPE = 64
MLA_ROPE = 32
MLA_V = 64
MLA_Q_RANK = 384
MLA_KV_RANK = 256
N_EXPERTS = 32
TOP_K = 4
D_EXPERT = 1024
SWIGLU_LIMIT = 7.0
SWIGLU_ALPHA = 1.702
ROPE_THETA = 10000.0
EPS = 1e-6
N_MOD = 6

LANES = 128
F32_SUBLANES = 8
BF16_SUBLANES = 16
VMEM_LIMIT = 56 * 1024 * 1024
LOG2E = 1.4426950408889634

_QKV0, _GATE0, _CQ0, _CKV0, _KR0, _WP = 0, 3072, 5120, 5504, 5760, 5888

_F32 = jnp.float32
_BF16 = jnp.bfloat16
_NT = (((1,), (1,)), ((), ()))


def _dot(a, b):
    return jnp.dot(a, b, preferred_element_type=_F32)


def _dot_nt(a, b):
    return lax.dot_general(a, b, _NT, preferred_element_type=_F32)


def _bf16_truncate(x):
    bits = lax.bitcast_convert_type(x, jnp.uint32) & jnp.uint32(0xFFFF0000)
    return lax.bitcast_convert_type(bits, _F32)


def _rms(x, w):
    return x * lax.rsqrt(jnp.mean(x * x, axis=-1, keepdims=True) + EPS) * w


def _const_spec(shape):
    nd = len(shape)
    return pl.BlockSpec(shape, lambda *_: (0,) * nd, pipeline_mode=pl.Buffered(1))


def _params(sem):
    return pltpu.CompilerParams(dimension_semantics=sem, vmem_limit_bytes=VMEM_LIMIT)


def _ada_kernel(c_ref, w_ref, b_ref, lq1_ref, lk1_ref, lq2_ref, lk2_ref, mod_ref, lam_ref, *, lambda_init):
    c = c_ref[...]
    s = c * jax.nn.sigmoid(c)
    mod_ref[...] = jnp.dot(s, w_ref[...], preferred_element_type=_F32,
                           precision=lax.Precision.HIGHEST) + b_ref[...]
    d1 = jnp.sum(lq1_ref[...] * lk1_ref[...], axis=-1, keepdims=True)
    d2 = jnp.sum(lq2_ref[...] * lk2_ref[...], axis=-1, keepdims=True)
    lam = jnp.exp(d1) - jnp.exp(d2) + lambda_init
    lam_ref[...] = jnp.broadcast_to(lam, lam_ref.shape)


def _ada(c_all, w_ada, b_ada, lq1, lk1, lq2, lk2, lambda_init):
    rows = c_all.shape[0]
    ncol = w_ada.shape[1]
    tn = D_MODEL
    vec = lambda r: r.reshape(1, -1)
    small = pl.BlockSpec((1, DA_HEAD_DIM), lambda j: (0, 0))
    return pl.pallas_call(
        functools.partial(_ada_kernel, lambda_init=lambda_init),
        grid=(ncol // tn,),
        in_specs=[pl.BlockSpec((rows, D_MODEL), lambda j: (0, 0)),
                  pl.BlockSpec((D_MODEL, tn), lambda j: (0, j)),
                  pl.BlockSpec((1, tn), lambda j: (0, j)),
                  small, small, small, small],
        out_specs=[pl.BlockSpec((rows, tn), lambda j: (0, j)),
                   pl.BlockSpec((8, LANES), lambda j: (0, 0))],
        out_shape=[jax.ShapeDtypeStruct((rows, ncol), _F32),
                   jax.ShapeDtypeStruct((8, LANES), _F32)],
        compiler_params=_params(("arbitrary",)),
        name="ada",
    )(c_all, w_ada, vec(b_ada), vec(lq1), vec(lk1), vec(lq2), vec(lk2))


def _rope128(x, cos, sin_signed, first_half, shift_lo, shift_hi):
    r = jnp.where(first_half, pltpu.roll(x, shift_lo, 1), pltpu.roll(x, shift_hi, 1))
    return x * cos + r * sin_signed


def _inproj_kernel(x_ref, mod_ref, n1_ref, wp_ref, qn_ref, kvn_ref, wuq_ref, wuk_ref, wuv_ref,
                   cosa_ref, sina_ref, cosm_ref, sinm_ref,
                   qda_ref, kda_ref, vtda_ref, qm_ref, km_ref, vtm_ref, ga_ref, gb_ref,
                   *, da_scale, mla_scale):
    x = x_ref[0]
    mod = mod_ref[0]
    sh1 = mod[:, 0:D_MODEL]
    sc1 = mod[:, D_MODEL:2 * D_MODEL]
    h = _rms(x, n1_ref[...]) * (1.0 + sc1) + sh1
    hb = h.astype(_BF16)

    lane = lax.broadcasted_iota(jnp.int32, (x.shape[0], LANES), 1)
    da_first = (lane % DA_HEAD_DIM) < (DA_HEAD_DIM // 2)
    pe_mid = MLA_NOPE + MLA_ROPE // 2
    mla_first = lane < pe_mid
    cosa, sina = cosa_ref[...], sina_ref[...]
    cosm, sinm = cosm_ref[...], sinm_ref[...]
    half = DA_HEAD_DIM // 2
    mhalf = MLA_ROPE // 2

    for pair in range(DA_HEADS // 2):
        c0 = 2 * pair * LANES
        zq = _dot(hb, wp_ref[:, _QKV0 + c0:_QKV0 + c0 + 2 * LANES])
        zk = _dot(hb, wp_ref[:, _QKV0 + D_MODEL + c0:_QKV0 + D_MODEL + c0 + 2 * LANES])
        zv = _dot(hb, wp_ref[:, _QKV0 + 2 * D_MODEL + c0:_QKV0 + 2 * D_MODEL + c0 + 2 * LANES])
        for t in range(2):
            hd = 2 * pair + t
            sl = slice(t * LANES, (t + 1) * LANES)
            q = _rope128(zq[:, sl], cosa, sina, da_first, LANES - half, half)
            qda_ref[0, hd] = (q * da_scale).astype(_BF16)
            k = _rope128(zk[:, sl], cosa, sina, da_first, LANES - half, half)
            kda_ref[0, hd] = k.astype(_BF16)
            vtda_ref[0, hd] = zv[:, sl].T.astype(_BF16)

    for blk in range(2 * D_MODEL // 512):
        z = _dot(hb, wp_ref[:, _GATE0 + blk * 512:_GATE0 + (blk + 1) * 512])
        g = jax.nn.sigmoid(z).astype(_BF16)
        if blk < D_MODEL // 512:
            ga_ref[0, :, blk * 512:(blk + 1) * 512] = g
        else:
            o = blk * 512 - D_MODEL
            gb_ref[0, :, o:o + 512] = g

    cq = _dot(hb, wp_ref[:, _CQ0:_CQ0 + MLA_Q_RANK])
    cqn = _rms(cq, qn_ref[...]).astype(_BF16)
    for pair in range(MLA_HEADS // 2):
        z = _dot(cqn, wuq_ref[:, 2 * pair * LANES:(2 * pair + 2) * LANES])
        for t in range(2):
            q = _rope128(z[:, t * LANES:(t + 1) * LANES], cosm, sinm, mla_first, LANES - mhalf, mhalf)
            qm_ref[0, 2 * pair + t] = (q * mla_scale).astype(_BF16)

    ckv = _dot(hb, wp_ref[:, _CKV0:_CKV0 + MLA_KV_RANK])
    ckvn = _rms(ckv, kvn_ref[...]).astype(_BF16)
    kr = _dot(hb, wp_ref[:, _KR0:_KR0 + LANES])
    kr = _rope128(kr, cosm, sinm, mla_first, LANES - mhalf, mhalf)
    for pair in range(MLA_HEADS // 2):
        zk = _dot(ckvn, wuk_ref[:, 2 * pair * LANES:(2 * pair + 2) * LANES])
        for t in range(2):
            km_ref[0, 2 * pair + t] = (zk[:, t * LANES:(t + 1) * LANES] + kr).astype(_BF16)
        vt = _dot(ckvn, wuv_ref[:, pair * LANES:(pair + 1) * LANES]).T.astype(_BF16)
        vtm_ref[0, 2 * pair] = vt[0:MLA_V]
        vtm_ref[0, 2 * pair + 1] = vt[MLA_V:2 * MLA_V]


def _inproj(x, mod3, n1, wp, qn, kvn, wuq, wuk, wuv, cosa, sina, cosm, sinm):
    b, s, _ = x.shape
    tm = min(512, s)
    bf = lambda *shape: jax.ShapeDtypeStruct(shape, _BF16)
    tok = pl.BlockSpec((1, tm, D_MODEL), lambda bi, i: (bi, i, 0))
    tab = pl.BlockSpec((tm, LANES), lambda bi, i: (i, 0))
    hrow = lambda nh: pl.BlockSpec((1, nh, tm, LANES), lambda bi, i: (bi, 0, i, 0))
    hcol = lambda nh, r: pl.BlockSpec((1, nh, r, tm), lambda bi, i: (bi, 0, 0, i))
    kern = functools.partial(_inproj_kernel,
                             da_scale=DA_HEAD_DIM ** -0.5 * LOG2E,
                             mla_scale=(MLA_NOPE + MLA_ROPE) ** -0.5 * LOG2E)
    return pl.pallas_call(
        kern,
        grid=(b, s // tm),
        in_specs=[tok,
                  pl.BlockSpec((1, 1, N_MOD * D_MODEL), lambda bi, i: (bi, 0, 0)),
                  _const_spec((1, D_MODEL)),
                  _const_spec((D_MODEL, _WP)),
                  _const_spec((1, MLA_Q_RANK)),
                  _const_spec((1, MLA_KV_RANK)),
                  _const_spec((MLA_Q_RANK, MLA_HEADS * LANES)),
                  _const_spec((MLA_KV_RANK, MLA_HEADS * LANES)),
                  _const_spec((MLA_KV_RANK, MLA_HEADS * MLA_V)),
                  tab, tab, tab, tab],
        out_specs=[hrow(DA_HEADS), hrow(DA_HEADS), hcol(DA_HEADS, DA_V_DIM),
                   hrow(MLA_HEADS), hrow(MLA_HEADS), hcol(MLA_HEADS, MLA_V), tok, tok],
        out_shape=[bf(b, DA_HEADS, s, LANES), bf(b, DA_HEADS, s, LANES), bf(b, DA_HEADS, DA_V_DIM, s),
                   bf(b, MLA_HEADS, s, LANES), bf(b, MLA_HEADS, s, LANES), bf(b, MLA_HEADS, MLA_V, s),
                   bf(b, s, D_MODEL), bf(b, s, D_MODEL)],
        compiler_params=_params(("parallel", "parallel")),
        name="inproj",
    )(x, mod3, n1, wp, qn, kvn, wuq, wuk, wuv, cosa, sina, cosm, sinm)


DA_KEY_SUB, DA_KEY_BLOCK = 1024, 2048
MLA_KEY_SUB, MLA_KEY_BLOCK = 1024, 2048
QUERY_TILE = 512


def _flash_items(n_heads, tk, key_sub, scores, vt_of, m_ref, acc_ref):
    sub = min(key_sub, tk)
    items = [(hd, pl.ds(c * sub, sub)) for c in range(tk // sub) for hd in range(n_heads)]

    def stage_a(hd, ks):
        st = scores(hd, ks)
        m_old = m_ref[hd]
        m_new = jnp.maximum(m_old, jnp.max(st, axis=0, keepdims=True))
        m_ref[hd] = m_new
        return st, m_new, jnp.exp2(m_old - m_new)

    def stage_c(hd, ks, p, alpha):
        vt = vt_of(hd, ks)
        ones = jnp.ones((BF16_SUBLANES, vt.shape[1]), _BF16)
        pv = _dot(jnp.concatenate([vt, ones], axis=0), p)
        acc_ref[hd] = alpha * acc_ref[hd] + pv

    a_out, b_out = {}, {}
    for t in range(len(items) + 2):
        if t < len(items):
            a_out[t] = stage_a(*items[t])
        if 0 <= t - 1 < len(items):
            st, m_new, alpha = a_out.pop(t - 1)
            b_out[t - 1] = (jnp.exp2(st - m_new).astype(_BF16), alpha)
        if 0 <= t - 2 < len(items):
            stage_c(*items[t - 2], *b_out.pop(t - 2))


def _da_kernel(lam_ref, q_ref, k_ref, vt_ref, sub_ref, o_ref, w_ref, m_ref, acc_ref, *, out_scale):
    j = pl.program_id(2)
    tq = q_ref.shape[2]

    @pl.when(j == 0)
    def _():
        lane = lax.broadcasted_iota(jnp.int32, (tq, LANES), 1)
        first = lane < DA_HEAD_DIM
        for hd in range(DA_HEADS):
            q = q_ref[0, hd]
            w_ref[2 * hd] = jnp.where(first, q, jnp.zeros_like(q))
            w_ref[2 * hd + 1] = jnp.where(first, jnp.zeros_like(q), q)
        m_ref[...] = jnp.full(m_ref.shape, -jnp.inf, _F32)
        acc_ref[...] = jnp.zeros(acc_ref.shape, _F32)

    _flash_items(2 * DA_HEADS, k_ref.shape[2], DA_KEY_SUB,
                 lambda sm, ks: _dot_nt(k_ref[0, sm // 2, ks, :], w_ref[sm]),
                 lambda sm, ks: vt_ref[0, sm // 2, :, ks], m_ref, acc_ref)

    @pl.when(j == pl.num_programs(2) - 1)
    def _():
        lam = lam_ref[0]
        for hd in range(DA_HEADS):
            a0, a1 = acc_ref[2 * hd], acc_ref[2 * hd + 1]
            o0 = a0[0:DA_V_DIM] / a0[DA_V_DIM:DA_V_DIM + 1]
            o1 = a1[0:DA_V_DIM] / a1[DA_V_DIM:DA_V_DIM + 1]
            od = (o0 - lam * o1).T
            o_ref[0, :, hd * LANES:(hd + 1) * LANES] = (_rms(od, sub_ref[...]) * out_scale).astype(o_ref.dtype)


def _da_attention(lam1, q, k, vt, subln, lambda_init):
    b, nh, s, _ = q.shape
    tq, tk = min(QUERY_TILE, s), min(DA_KEY_BLOCK, s)
    return pl.pallas_call(
        functools.partial(_da_kernel, out_scale=1.0 - lambda_init),
        grid=(b, s // tq, s // tk),
        in_specs=[pl.BlockSpec(memory_space=pltpu.SMEM),
                  pl.BlockSpec((1, nh, tq, LANES), lambda bi, i, j: (bi, 0, i, 0)),
                  pl.BlockSpec((1, nh, tk, LANES), lambda bi, i, j: (bi, 0, j, 0)),
                  pl.BlockSpec((1, nh, DA_V_DIM, tk), lambda bi, i, j: (bi, 0, 0, j)),
                  pl.BlockSpec((1, DA_V_DIM), lambda bi, i, j: (0, 0))],
        out_specs=pl.BlockSpec((1, tq, D_MODEL), lambda bi, i, j: (bi, i, 0)),
        out_shape=jax.ShapeDtypeStruct((b, s, D_MODEL), _BF16),
        scratch_shapes=[pltpu.VMEM((2 * nh, tq, LANES), _BF16),
                        pltpu.VMEM((2 * nh, 1, tq), _F32),
                        pltpu.VMEM((2 * nh, DA_V_DIM + BF16_SUBLANES, tq), _F32)],
        compiler_params=_params(("parallel", "parallel", "arbitrary")),
        name="da_attention",
    )(lam1, q, k, vt, subln)


def _mla_kernel(q_ref, k_ref, vt_ref, o_ref, m_ref, acc_ref):
    j = pl.program_id(2)

    @pl.when(j == 0)
    def _():
        m_ref[...] = jnp.full(m_ref.shape, -jnp.inf, _F32)
        acc_ref[...] = jnp.zeros(acc_ref.shape, _F32)

    _flash_items(MLA_HEADS, k_ref.shape[2], MLA_KEY_SUB,
                 lambda hd, ks: _dot_nt(k_ref[0, hd, ks, :], q_ref[0, hd]),
                 lambda hd, ks: vt_ref[0, hd, :, ks], m_ref, acc_ref)

    @pl.when(j == pl.num_programs(2) - 1)
    def _():
        for pair in range(MLA_HEADS // 2):
            outs = []
            for hd in (2 * pair, 2 * pair + 1):
                a = acc_ref[hd]
                outs.append(a[0:MLA_V] / a[MLA_V:MLA_V + 1])
            o_ref[0, :, pair * LANES:(pair + 1) * LANES] = jnp.concatenate(outs, axis=0).T.astype(o_ref.dtype)


def _mla_attention(q, k, vt):
    b, nh, s, _ = q.shape
    tq, tk = min(QUERY_TILE, s), min(MLA_KEY_BLOCK, s)
    return pl.pallas_call(
        _mla_kernel,
        grid=(b, s // tq, s // tk),
        in_specs=[pl.BlockSpec((1, nh, tq, LANES), lambda bi, i, j: (bi, 0, i, 0)),
                  pl.BlockSpec((1, nh, tk, LANES), lambda bi, i, j: (bi, 0, j, 0)),
                  pl.BlockSpec((1, nh, MLA_V, tk), lambda bi, i, j: (bi, 0, 0, j))],
        out_specs=pl.BlockSpec((1, tq, D_MODEL), lambda bi, i, j: (bi, i, 0)),
        out_shape=jax.ShapeDtypeStruct((b, s, D_MODEL), _BF16),
        scratch_shapes=[pltpu.VMEM((nh, 1, tq), _F32),
                        pltpu.VMEM((nh, MLA_V + BF16_SUBLANES, tq), _F32)],
        compiler_params=_params(("parallel", "parallel", "arbitrary")),
        name="mla_attention",
    )(q, k, vt)


def _outp_kernel(x_ref, mod_ref, oa_ref, ob_ref, ga_ref, gb_ref, wba_ref, wbb_ref, wo_ref,
                 npm_ref, npf_ref, wr_ref, br_ref,
                 x1_ref, h2_ref, idx_ref, pos_ref, w_ref, cnt_ref, carry_ref):
    first = jnp.logical_and(pl.program_id(0) == 0, pl.program_id(1) == 0)

    @pl.when(first)
    def _():
        carry_ref[...] = jnp.zeros(carry_ref.shape, _F32)

    mod = mod_ref[0]
    g1 = mod[:, 2 * D_MODEL:3 * D_MODEL]
    sh2 = mod[:, 3 * D_MODEL:4 * D_MODEL]
    sc2 = mod[:, 4 * D_MODEL:5 * D_MODEL]
    merged = (ga_ref[0].astype(_F32) * _dot(oa_ref[0], wba_ref[...])
              + gb_ref[0].astype(_F32) * _dot(ob_ref[0], wbb_ref[...]))
    mix = _dot(merged.astype(_BF16), wo_ref[...])
    x1 = x_ref[0] + g1 * _rms(mix, npm_ref[...])
    x1_ref[0] = x1
    h2 = _rms(x1, npf_ref[...]) * (1.0 + sc2) + sh2
    h2_ref[0] = h2

    h_hi = _bf16_truncate(h2)
    h_lo = (h2 - h_hi).astype(_BF16)
    h_hi = h_hi.astype(_BF16)
    logits = _dot(h_hi, wr_ref[0]) + (_dot(h_lo, wr_ref[0]) + _dot(h_hi, wr_ref[1])) + br_ref[...]
    tm = logits.shape[0]
    lane = lax.broadcasted_iota(jnp.int32, logits.shape, 1)
    vals = logits
    sels, idxs, exps = [], [], []
    top = None
    for _ in range(TOP_K):
        mx = jnp.max(vals, axis=-1, keepdims=True)
        idx = jnp.min(jnp.where(vals == mx, lane, LANES), axis=-1, keepdims=True)
        sel = lane == idx
        top = mx if top is None else top
        sels.append(sel)
        idxs.append(idx)
        exps.append(jnp.exp(mx - top))
        vals = jnp.where(sel, -jnp.inf, vals)
    den = exps[0] + exps[1] + exps[2] + exps[3]

    onehot = jnp.zeros(logits.shape, _F32)
    for sel in sels:
        onehot = onehot + jnp.where(sel, 1.0, 0.0)
    r_i = lax.broadcasted_iota(jnp.int32, (tm, tm), 0)
    c_i = lax.broadcasted_iota(jnp.int32, (tm, tm), 1)
    earlier = jnp.where(r_i > c_i, 1.0, 0.0).astype(_BF16)
    rank = _dot(earlier, onehot.astype(_BF16)) + carry_ref[0:1, :]
    idx4 = jnp.zeros(logits.shape, jnp.int32)
    pos4 = jnp.zeros(logits.shape, jnp.int32)
    w4 = jnp.zeros(logits.shape, _F32)
    for r in range(TOP_K):
        pos = jnp.sum(jnp.where(sels[r], rank, 0.0), axis=-1, keepdims=True)
        idx4 = jnp.where(lane == r, idxs[r], idx4)
        pos4 = jnp.where(lane == r, pos.astype(jnp.int32), pos4)
        w4 = jnp.where(lane == r, exps[r] / den, w4)
    idx_ref[0] = idx4
    pos_ref[0] = pos4
    w_ref[0] = w4
    carry_ref[...] = carry_ref[...] + jnp.sum(onehot, axis=0, keepdims=True)
    cnt_ref[...] = carry_ref[...]


def _outp(x, mod3, oa, ob, ga, gb, wba, wbb, wo, npm, npf, wr, br):
    b, s, _ = x.shape
    tm = min(512, s)
    tok = pl.BlockSpec((1, tm, D_MODEL), lambda bi, i: (bi, i, 0))
    meta = pl.BlockSpec((1, tm, LANES), lambda bi, i: (bi, i, 0))
    sq = _const_spec((D_MODEL, D_MODEL))
    row = _const_spec((1, D_MODEL))
    return pl.pallas_call(
        _outp_kernel,
        grid=(b, s // tm),
        in_specs=[tok, pl.BlockSpec((1, 1, N_MOD * D_MODEL), lambda bi, i: (bi, 0, 0)),
                  tok, tok, tok, tok, sq, sq, sq, row, row,
                  _const_spec((2, D_MODEL, LANES)), _const_spec((1, LANES))],
        out_specs=[tok, tok, meta, meta, meta, pl.BlockSpec((8, LANES), lambda bi, i: (0, 0))],
        out_shape=[jax.ShapeDtypeStruct((b, s, D_MODEL), _F32),
                   jax.ShapeDtypeStruct((b, s, D_MODEL), _F32),
                   jax.ShapeDtypeStruct((b, s, LANES), jnp.int32),
                   jax.ShapeDtypeStruct((b, s, LANES), jnp.int32),
                   jax.ShapeDtypeStruct((b, s, LANES), _F32),
                   jax.ShapeDtypeStruct((8, LANES), _F32)],
        scratch_shapes=[pltpu.VMEM((8, LANES), _F32)],
        compiler_params=_params(("arbitrary", "arbitrary")),
        name="outp",
    )(x, mod3, oa, ob, ga, gb, wba, wbb, wo, npm, npf, wr, br)


MOE_ROW_TILE = 512
ROUTE_TILE = 512


def _scatter_kernel(fill_ref, dest_ref, h_ref, xs_ref, zero_ref, sem, zero_sem):
    tm = h_ref.shape[0]
    tr = zero_ref.shape[0]

    @pl.when(pl.program_id(0) == 0)
    def _():
        zero_ref[...] = jnp.zeros(zero_ref.shape, _F32)

        def fill_from(row):
            fill = pltpu.make_async_copy(zero_ref, xs_ref.at[pl.ds(row, tr)], zero_sem)
            fill.start()
            fill.wait()

        for e in range(N_EXPERTS):
            fill_from(pl.multiple_of(fill_ref[e], F32_SUBLANES))

        def fill_tile(g, carry):
            fill_from(pl.multiple_of(g * tr, tr))
            return carry

        lax.fori_loop(fill_ref[N_EXPERTS], xs_ref.shape[0] // tr, fill_tile, 0)

    def issue(t, carry):
        for r in range(TOP_K):
            d = dest_ref[TOP_K * t + r]
            pltpu.make_async_copy(h_ref.at[pl.ds(t, 1)], xs_ref.at[pl.ds(d, 1)], sem).start()
        return carry

    lax.fori_loop(0, tm, issue, 0)
    for _ in range(TOP_K):
        pltpu.make_async_copy(h_ref, xs_ref.at[pl.ds(0, tm)], sem).wait()


def _scatter_rows(fill, dest, h2, n_rows):
    t = h2.shape[0]
    tm = min(ROUTE_TILE, t)
    return pl.pallas_call(
        _scatter_kernel,
        grid_spec=pltpu.PrefetchScalarGridSpec(
            num_scalar_prefetch=1,
            grid=(t // tm,),
            in_specs=[pl.BlockSpec((TOP_K * tm,), lambda i, fill: (i,), memory_space=pltpu.SMEM),
                      pl.BlockSpec((tm, D_MODEL), lambda i, fill: (i, 0))],
            out_specs=pl.BlockSpec(memory_space=pl.ANY),
            scratch_shapes=[pltpu.VMEM((MOE_ROW_TILE, D_MODEL), _F32),
                            pltpu.SemaphoreType.DMA(()), pltpu.SemaphoreType.DMA(())]),
        out_shape=jax.ShapeDtypeStruct((n_rows, D_MODEL), _F32),
        compiler_params=_params(("arbitrary",)),
        name="moe_scatter",
    )(fill, dest, h2)


def _experts_kernel(te_ref, xs_ref, wgu_ref, bgu_ref, wd_ref, bd_ref, ys_ref):
    del te_ref
    gu = _dot(xs_ref[...].astype(_BF16), wgu_ref[0].astype(_BF16)) + bgu_ref[0]
    glu = jnp.minimum(gu[:, :D_EXPERT], SWIGLU_LIMIT)
    lin = jnp.clip(gu[:, D_EXPERT:], -SWIGLU_LIMIT, SWIGLU_LIMIT)
    act = glu * jax.nn.sigmoid(SWIGLU_ALPHA * glu) * (lin + 1.0)
    ys_ref[...] = _dot(act.astype(_BF16), wd_ref[0].astype(_BF16)) + bd_ref[0]


def _experts(tile_expert, xs, wgu, bgu, wd, bd):
    n_rows = xs.shape[0]
    tr = MOE_ROW_TILE
    rows = pl.BlockSpec((tr, D_MODEL), lambda g, te: (g, 0))
    return pl.pallas_call(
        _experts_kernel,
        grid_spec=pltpu.PrefetchScalarGridSpec(
            num_scalar_prefetch=1,
            grid=(n_rows // tr,),
            in_specs=[rows,
                      pl.BlockSpec((1, D_MODEL, 2 * D_EXPERT), lambda g, te: (te[g], 0, 0)),
                      pl.BlockSpec((1, 1, 2 * D_EXPERT), lambda g, te: (te[g], 0, 0)),
                      pl.BlockSpec((1, D_EXPERT, D_MODEL), lambda g, te: (te[g], 0, 0)),
                      pl.BlockSpec((1, 1, D_MODEL), lambda g, te: (te[g], 0, 0))],
            out_specs=rows),
        out_shape=jax.ShapeDtypeStruct((n_rows, D_MODEL), _F32),
        compiler_params=_params(("arbitrary",)),
        name="moe_experts",
    )(tile_expert, xs, wgu, bgu, wd, bd)


def _combine_kernel(dest_ref, ys_ref, w_ref, x1_ref, mod_ref, npo_ref, o_ref, buf_ref, sem):
    tm = x1_ref.shape[1]

    def issue(t, carry):
        for r in range(TOP_K):
            d = dest_ref[TOP_K * t + r]
            pltpu.make_async_copy(ys_ref.at[pl.ds(d, 1)], buf_ref.at[r, pl.ds(t, 1)], sem).start()
        return carry

    lax.fori_loop(0, tm, issue, 0)
    for r in range(TOP_K):
        pltpu.make_async_copy(ys_ref.at[pl.ds(0, tm)], buf_ref.at[r], sem).wait()
    w = w_ref[0]
    ff = w[:, 0:1] * buf_ref[0]
    for r in range(1, TOP_K):
        ff = ff + w[:, r:r + 1] * buf_ref[r]
    g2 = mod_ref[0][:, 5 * D_MODEL:6 * D_MODEL]
    o_ref[0] = x1_ref[0] + g2 * _rms(ff, npo_ref[...])


def _combine(dest, ys, w4, x1, mod3, npo):
    b, s, _ = x1.shape
    tm = min(ROUTE_TILE, s)
    nt = s // tm
    tok = pl.BlockSpec((1, tm, D_MODEL), lambda bi, i: (bi, i, 0))
    return pl.pallas_call(
        _combine_kernel,
        grid=(b, nt),
        in_specs=[pl.BlockSpec((TOP_K * tm,), lambda bi, i: (bi * nt + i,), memory_space=pltpu.SMEM),
                  pl.BlockSpec(memory_space=pl.ANY),
                  pl.BlockSpec((1, tm, LANES), lambda bi, i: (bi, i, 0)),
                  tok, pl.BlockSpec((1, 1, N_MOD * D_MODEL), lambda bi, i: (bi, 0, 0)),
                  pl.BlockSpec((1, D_MODEL), lambda bi, i: (0, 0))],
        out_specs=tok,
        out_shape=jax.ShapeDtypeStruct((b, s, D_MODEL), _F32),
        scratch_shapes=[pltpu.VMEM((TOP_K, tm, D_MODEL), _F32), pltpu.SemaphoreType.DMA(())],
        compiler_params=_params(("arbitrary", "arbitrary")),
        name="moe_combine",
    )(dest, ys, w4, x1, mod3, npo)


def _moe(h2, idx4, pos4, w4, counts, wgu, bgu, wd, bd, x1, mod3, npo):
    b, s, _ = h2.shape
    t = b * s
    tr = MOE_ROW_TILE
    n_tiles = (t * TOP_K) // tr + N_EXPERTS
    cnt = counts[0, :N_EXPERTS].astype(jnp.int32)
    padded = ((cnt + tr - 1) // tr) * tr
    ends = jnp.cumsum(padded)
    starts = ends - padded
    sel = idx4[..., :TOP_K, None] == jnp.arange(N_EXPERTS, dtype=jnp.int32)
    dest = (jnp.sum(jnp.where(sel, starts, 0), axis=-1) + pos4[..., :TOP_K]).reshape(t * TOP_K)
    tile_row0 = jnp.arange(n_tiles, dtype=jnp.int32) * tr
    tile_expert = jnp.sum((ends[None, :] <= tile_row0[:, None]).astype(jnp.int32), axis=-1)
    tile_expert = jnp.minimum(tile_expert, N_EXPERTS - 1)
    fill = jnp.concatenate([(starts + cnt) // F32_SUBLANES * F32_SUBLANES, ends[-1:] // tr])
    xs = _scatter_rows(fill, dest, h2.reshape(t, D_MODEL), n_tiles * tr)
    ys = _experts(tile_expert, xs, wgu, bgu, wd, bd)
    return _combine(dest, ys, w4, x1, mod3, npo)


def _rope_tables(seq, dim):
    inv = 1.0 / (ROPE_THETA ** (jnp.arange(0, dim, 2, dtype=_F32) / dim))
    ang = jnp.arange(seq, dtype=_F32)[:, None] * inv[None, :]
    ang = jnp.concatenate([ang, ang], axis=-1)
    return jnp.cos(ang), jnp.sin(ang)


def _tables(seq):
    cos_a, sin_a = _rope_tables(seq, DA_HEAD_DIM)
    half = DA_HEAD_DIM // 2
    sin_a = jnp.concatenate([-sin_a[:, :half], sin_a[:, half:]], axis=-1)
    cosa = jnp.tile(cos_a, (1, LANES // DA_HEAD_DIM))
    sina = jnp.tile(sin_a, (1, LANES // DA_HEAD_DIM))
    cos_b, sin_b = _rope_tables(seq, MLA_ROPE)
    mh = MLA_ROPE // 2
    sin_b = jnp.concatenate([-sin_b[:, :mh], sin_b[:, mh:]], axis=-1)
    pad = LANES - MLA_NOPE - MLA_ROPE
    cosm = jnp.concatenate([jnp.ones((seq, MLA_NOPE), _F32), cos_b, jnp.ones((seq, pad), _F32)], axis=-1)
    sinm = jnp.concatenate([jnp.zeros((seq, MLA_NOPE), _F32), sin_b, jnp.zeros((seq, pad), _F32)], axis=-1)
    return cosa, sina, cosm, sinm


def _pack_weights(w_in, mla_w_uq, mla_w_ukv):
    o = [0, 1024, 2048, 3072, 3456, 3712, 3744, 4768, 5792]
    qkv, cq, ckv, kr, gates = w_in[:, o[0]:o[3]], w_in[:, o[3]:o[4]], w_in[:, o[4]:o[5]], w_in[:, o[5]:o[6]], w_in[:, o[6]:o[8]]
    kr128 = jnp.pad(kr, ((0, 0), (MLA_NOPE, LANES - MLA_NOPE - MLA_ROPE)))
    wp = jnp.concatenate([qkv, gates, cq, ckv, kr128], axis=1).astype(_BF16)
    qk_dim = MLA_NOPE + MLA_ROPE
    wuq = mla_w_uq.reshape(MLA_Q_RANK, MLA_HEADS, qk_dim)
    wuq = jnp.pad(wuq, ((0, 0), (0, 0), (0, LANES - qk_dim))).reshape(MLA_Q_RANK, MLA_HEADS * LANES).astype(_BF16)
    wukv = mla_w_ukv.reshape(MLA_KV_RANK, MLA_HEADS, MLA_NOPE + MLA_V)
    wuk = jnp.pad(wukv[:, :, :MLA_NOPE], ((0, 0), (0, 0), (0, LANES - MLA_NOPE)))
    wuk = wuk.reshape(MLA_KV_RANK, MLA_HEADS * LANES).astype(_BF16)
    wuv = wukv[:, :, MLA_NOPE:].reshape(MLA_KV_RANK, MLA_HEADS * MLA_V).astype(_BF16)
    return wp, wuq, wuk, wuv


def _split_hi_lo(w):
    hi = _bf16_truncate(w)
    return jnp.stack([hi.astype(_BF16), (w - hi).astype(_BF16)])


def _layer(x, mod, lam1, lambda_init, p):
    b, s, _ = x.shape
    mod3 = mod.reshape(b, 1, N_MOD * D_MODEL)
    cosa, sina, cosm, sinm = _tables(s)
    qda, kda, vtda, qm, km, vtm, ga, gb = _inproj(
        x, mod3, p["n_pre_mix"], p["wp"], p["q_norm"], p["kv_norm"], p["wuq"], p["wuk"], p["wuv"],
        cosa, sina, cosm, sinm)
    oa = _da_attention(lam1, qda, kda, vtda, p["da_subln"], lambda_init)
    ob = _mla_attention(qm, km, vtm)
    x1, h2, idx4, pos4, w4, counts = _outp(x, mod3, oa, ob, ga, gb, p["wba"], p["wbb"], p["wo"],
                                           p["n_post_mix"], p["n_pre_ffn"], p["wr"], p["br"])
    return _moe(h2, idx4, pos4, w4, counts, p["wgu"], p["bgu"], p["wd"], p["bd"], x1, mod3, p["n_post_ffn"])


def kernel(x_prompt, x_sample, c_prompt, c_sample, w_ada, b_ada, norm_pre_mix, norm_post_mix, norm_pre_ffn, norm_post_ffn, w_in, da_lambda_q1, da_lambda_k1, da_lambda_q2, da_lambda_k2, da_subln, mla_q_norm, mla_kv_norm, mla_w_uq, mla_w_ukv, w_branch_a, w_branch_b, w_out, w_router, b_router, w_gate_up, b_gate_up, w_down, b_down):
    depth = w_ada.shape[0]
    xp, xs = x_prompt, x_sample
    bp, bs = c_prompt.shape[0], c_sample.shape[0]
    rows = -(-(bp + bs) // 8) * 8
    c_all = jnp.concatenate([c_prompt, c_sample, jnp.zeros((rows - bp - bs, D_MODEL), _F32)], axis=0)
    row = lambda v: v.reshape(1, -1)
    for l in range(depth):
        lambda_init = 0.8 - 0.6 * math.exp(-0.3 * l)
        mod, lam = _ada(c_all, w_ada[l], b_ada[l], da_lambda_q1[l], da_lambda_k1[l],
                        da_lambda_q2[l], da_lambda_k2[l], lambda_init)
        lam1 = lam[0, 0:1]
        wp, wuq, wuk, wuv = _pack_weights(w_in[l], mla_w_uq[l], mla_w_ukv[l])
        p = dict(
            n_pre_mix=row(norm_pre_mix[l]), n_post_mix=row(norm_post_mix[l]),
            n_pre_ffn=row(norm_pre_ffn[l]), n_post_ffn=row(norm_post_ffn[l]),
            wp=wp, wuq=wuq, wuk=wuk, wuv=wuv,
            q_norm=row(mla_q_norm[l]), kv_norm=row(mla_kv_norm[l]), da_subln=row(da_subln[l]),
            wba=w_branch_a[l].astype(_BF16), wbb=w_branch_b[l].astype(_BF16), wo=w_out[l].astype(_BF16),
            wr=_split_hi_lo(jnp.pad(w_router[l], ((0, 0), (0, LANES - N_EXPERTS)))),
            br=jnp.pad(row(b_router[l]), ((0, 0), (0, LANES - N_EXPERTS)), constant_values=-1e30),
            wgu=w_gate_up[l], bgu=b_gate_up[l].reshape(N_EXPERTS, 1, 2 * D_EXPERT),
            wd=w_down[l], bd=b_down[l].reshape(N_EXPERTS, 1, D_MODEL),
        )
        xp = _layer(xp, mod[0:bp], lam1, lambda_init, p)
        xs = _layer(xs, mod[bp:bp + bs], lam1, lambda_init, p)
    return (xp, xs)
```
